```python
import math
import jax, jax.numpy as jnp
from jax import lax
import numpy as np

D_MODEL = 4096
BATCH = 4
SEQ = 2048
DEPTH = 2
DEC_BATCH = 128
DEC_SEQ = 4
PAST_LEN = 16384
PAGE_SIZE = 128

MIX_WIDTH = D_MODEL
GLA_WIDTH = MIX_WIDTH // 2
GLA_HEADS = 4
GLA_DV = GLA_WIDTH // GLA_HEADS
GLA_DK = GLA_DV // 2
GLA_QK = GLA_HEADS * GLA_DK
GLA_GATE_RANK = 16
GLA_GATE_TAU = 16.0
GLA_CHUNK = 16
RWKV_WIDTH = MIX_WIDTH - GLA_WIDTH
RWKV_HEAD = 64
RWKV_HEADS = RWKV_WIDTH // RWKV_HEAD
RWKV_W_LORA = max(32, round(1.8 * D_MODEL ** 0.5 / 32) * 32)
RWKV_A_LORA = max(32, round(1.8 * D_MODEL ** 0.5 / 32) * 32)
RWKV_V_LORA = max(32, round(1.3 * D_MODEL ** 0.5 / 32) * 32)
RWKV_G_LORA = max(32, round(0.6 * D_MODEL ** 0.8 / 32) * 32)
GN_EPS = 64e-5
P_GLA = 2 * GLA_QK + 2 * GLA_WIDTH + GLA_GATE_RANK
P_RWKV_FIRST = 3 * RWKV_WIDTH + RWKV_W_LORA + RWKV_A_LORA + RWKV_G_LORA
P_RWKV_REST = P_RWKV_FIRST + RWKV_V_LORA
FFN_DIM = 11008
MOE_EXPERTS = 8
MOE_TOP_K = 2
MOE_FFN_DIM = 14336
NORM_EPS = 1e-6
HEAD_NORM_EPS = 1e-5

kernel_name = "hymba_gla_rwkv7_moe_step"

F32 = jnp.float32


def rmsnorm(x, g):
    xf = x.astype(F32)
    y = xf * lax.rsqrt(jnp.mean(xf * xf, axis=-1, keepdims=True) + NORM_EPS)
    return y.astype(x.dtype) * g


def _split_last(t, sizes):
    out, start = [], 0
    for s in sizes:
        out.append(t[..., start:start + s])
        start += s
    return out


def gla_chunked(q, k, v, log_a, s0):
    B, T, H, K = q.shape
    V = v.shape[-1]
    C = math.gcd(T, GLA_CHUNK)
    n = T // C

    def blocks(t):
        return t.astype(F32).reshape(B, n, C, H, t.shape[-1]).transpose(1, 0, 3, 2, 4)

    mask = jnp.tril(jnp.ones((C, C), dtype=bool))

    def step(S, blk):
        qc, kc, vc, gc = blk
        b = jnp.cumsum(gc, axis=2)
        b_last = b[:, :, -1:, :]
        q_e = qc * jnp.exp(b)
        A = jnp.einsum('bhik,bhjk->bhij', q_e, kc * jnp.exp(-b))
        A = jnp.where(mask, A, 0.0)
        o = jnp.einsum('bhij,bhjv->bhiv', A, vc) + jnp.einsum('bhik,bhkv->bhiv', q_e, S)
        S = jnp.exp(b_last[:, :, 0, :, None]) * S + jnp.einsum(
            'bhjk,bhjv->bhkv', kc * jnp.exp(b_last - b), vc)
        return S, o

    s_final, o = lax.scan(step, s0.astype(F32), (blocks(q), blocks(k), blocks(v), blocks(log_a)))
    o = o.transpose(1, 0, 3, 2, 4).reshape(B, T, H, V)
    return o, s_final


def rwkv7_scan(r, w, k, v, a, b, s0):
    def step(S, inp):
        r_t, w_t, k_t, v_t, a_t, b_t = inp
        sa = jnp.einsum('bhvk,bhk->bhv', S, a_t)
        S = S * w_t[:, :, None, :] + sa[..., None] * b_t[:, :, None, :] + v_t[..., None] * k_t[:, :, None, :]
        return S, jnp.einsum('bhvk,bhk->bhv', S, r_t)

    xs = tuple(jnp.moveaxis(t.astype(F32), 1, 0) for t in (r, w, k, v, a, b))
    s_final, y = lax.scan(step, s0.astype(F32), xs)
    return jnp.moveaxis(y, 0, 1), s_final


def mixer(h, h_prev, s_gla, s_rwkv, v_first, w_in, mu, gla_w2, gla_b, gla_norm,
          rw_w2, rw_w0, rw_a2, rw_a0, rw_g2, rw_kk, rw_ka, rw_rk, rw_gn_w, rw_gn_b,
          rw_v2, rw_v0, w_out):
    B, T, _ = h.shape
    dt = h.dtype
    p = jnp.einsum('btd,dp->btp', h, w_in)
    p_gla, p_rw = p[..., :P_GLA], p[..., P_GLA:]
    p_prev = jnp.einsum('bd,dp->bp', h_prev, w_in[:, P_GLA:])
    p_rw_shifted = jnp.concatenate([p_prev[:, None, :], p_rw[:, :-1, :]], axis=1)
    p_rw = p_rw + mu * (p_rw_shifted - p_rw)

    q, k, v, g_out, lr = _split_last(p_gla, (GLA_QK, GLA_QK, GLA_WIDTH, GLA_WIDTH, GLA_GATE_RANK))
    log_a = jax.nn.log_sigmoid((jnp.einsum('btr,rk->btk', lr, gla_w2) + gla_b).astype(F32)) / GLA_GATE_TAU
    q = q.reshape(B, T, GLA_HEADS, GLA_DK) * (GLA_DK ** -0.5)
    k = k.reshape(B, T, GLA_HEADS, GLA_DK)
    v = v.reshape(B, T, GLA_HEADS, GLA_DV)
    log_a = log_a.reshape(B, T, GLA_HEADS, GLA_DK)
    o, s_gla_new = gla_chunked(q, k, v, log_a, s_gla)
    o = o * lax.rsqrt(jnp.mean(o * o, axis=-1, keepdims=True) + HEAD_NORM_EPS)
    o_gla = o.reshape(B, T, GLA_WIDTH).astype(dt) * gla_norm * jax.nn.silu(g_out)

    sizes = [RWKV_WIDTH, RWKV_WIDTH, RWKV_WIDTH, RWKV_W_LORA, RWKV_A_LORA, RWKV_G_LORA]
    if v_first is not None:
        sizes.append(RWKV_V_LORA)
    parts = _split_last(p_rw, sizes)
    r, kr, vr, xw, xa, xg = parts[:6]
    w_log = -jax.nn.softplus(-(rw_w0 + jnp.tanh(xw) @ rw_w2).astype(F32)) - 0.5
    decay = jnp.exp(-jnp.exp(w_log))
    a = jax.nn.sigmoid(rw_a0 + xa @ rw_a2)
    g = jax.nn.sigmoid(xg) @ rw_g2
    if v_first is None:
        v_first = vr
    else:
        vr = vr + (v_first - vr) * jax.nn.sigmoid(rw_v0 + parts[6] @ rw_v2)

    def heads(t):
        return t.reshape(B, T, RWKV_HEADS, RWKV_HEAD).astype(F32)

    kk = heads(kr * rw_kk)
    kk = kk * lax.rsqrt(jnp.maximum(jnp.sum(kk * kk, axis=-1, keepdims=True), 1e-24))
    kr = kr * (1 + (a - 1) * rw_ka)
    rh, kh, vh, ah = heads(r), heads(kr), heads(vr), heads(a)
    y, s_rw_new = rwkv7_scan(rh, heads(decay), kh, vh, -kk, kk * ah, s_rwkv)
    mean = jnp.mean(y, axis=-1, keepdims=True)
    var = jnp.mean(jnp.square(y - mean), axis=-1, keepdims=True)
    y = (y - mean) * lax.rsqrt(var + GN_EPS)
    y = y * rw_gn_w.reshape(RWKV_HEADS, RWKV_HEAD) + rw_gn_b.reshape(RWKV_HEADS, RWKV_HEAD)
    y = y + jnp.sum(rh * kh * rw_rk, axis=-1, keepdims=True) * vh
    o_rw = y.reshape(B, T, RWKV_WIDTH).astype(dt) * g

    out = jnp.einsum('btm,md->btd', jnp.concatenate([o_gla, o_rw], axis=-1), w_out)
    return out, s_gla_new, s_rw_new, v_first


def swiglu(h, wg, wu, wd):
    return (jax.nn.silu(h @ wg) * (h @ wu)) @ wd


def moe_swiglu(h, router, wg, wu, wd):
    logits = jnp.einsum('btd,de->bte', h, router).astype(F32)
    top_v, top_i = lax.top_k(logits, MOE_TOP_K)
    gates = jax.nn.softmax(top_v, axis=-1)
    combine = jnp.einsum('btk,btke->bte', gates,
                         jax.nn.one_hot(top_i, MOE_EXPERTS, dtype=F32)).astype(h.dtype)
    y = jnp.zeros_like(h)
    for e in range(MOE_EXPERTS):
        y = y + combine[..., e:e + 1] * swiglu(h, wg[e], wu[e], wd[e])
    return y


def trunk(x, st_gla, st_rwkv, st_shift, p):
    new_gla, new_rw, new_shift = [], [], []
    v_first = None
    for l in range(DEPTH):
        h = rmsnorm(x, p['norm_mix'][l])
        if l == 0:
            w_in, mu, v2, v0 = p['w_in_first'], p['mu_first'], None, None
        else:
            w_in, mu = p['w_in_rest'][l - 1], p['mu_rest'][l - 1]
            v2, v0 = p['rw_v2'][l - 1], p['rw_v0'][l - 1]
        mix, sg, sr, v_first = mixer(
            h, st_shift[l], st_gla[l], st_rwkv[l], v_first, w_in, mu,
            p['gla_w2'][l], p['gla_b'][l], p['gla_norm'][l],
            p['rw_w2'][l], p['rw_w0'][l], p['rw_a2'][l], p['rw_a0'][l], p['rw_g2'][l],
            p['rw_kk'][l], p['rw_ka'][l], p['rw_rk'][l], p['rw_gn_w'][l], p['rw_gn_b'][l],
            v2, v0, p['w_out'][l])
        new_gla.append(sg.astype(st_gla.dtype))
        new_rw.append(sr.astype(st_rwkv.dtype))
        new_shift.append(h[:, -1, :].astype(st_shift.dtype))
        x = x + mix
        h = rmsnorm(x, p['norm_ffn'][l])
        if l % 2 == 0:
            i = l // 2
            f = swiglu(h, p['ffn_w_gate'][i], p['ffn_w_up'][i], p['ffn_w_down'][i])
        else:
            i = l // 2
            f = moe_swiglu(h, p['moe_router'][i], p['moe_w_gate'][i], p['moe_w_up'][i], p['moe_w_down'][i])
        x = x + f
    y = rmsnorm(x, p['norm_final'])
    return y, jnp.stack(new_gla), jnp.stack(new_rw), jnp.stack(new_shift)


def setup_inputs(seed: int = 0) -> dict:
    key = jax.random.key(seed)
    ks = iter(jax.random.split(key, 40))
    n_dense = (DEPTH + 1) // 2
    n_moe = DEPTH // 2

    def nrm(shape, scale):
        return jax.random.normal(next(ks), shape, F32) * scale

    def gain(shape):
        return 1.0 + nrm(shape, 0.02)

    return {
        'x_prompt': nrm((BATCH, SEQ, D_MODEL), 1.0),
        'x_sample': nrm((DEC_BATCH, DEC_SEQ, D_MODEL), 1.0),
        'state_gla': nrm((DEPTH, DEC_BATCH, GLA_HEADS, GLA_DK, GLA_DV), 0.1),
        'state_rwkv': nrm((DEPTH, DEC_BATCH, RWKV_HEADS, RWKV_HEAD, RWKV_HEAD), 0.1),
        'state_shift': nrm((DEPTH, DEC_BATCH, D_MODEL), 1.0),
        'norm_mix': gain((DEPTH, D_MODEL)),
        'norm_ffn': gain((DEPTH, D_MODEL)),
        'norm_final': gain((D_MODEL,)),
        'w_in_first': nrm((D_MODEL, P_GLA + P_RWKV_FIRST), D_MODEL ** -0.5),
        'w_in_rest': nrm((DEPTH - 1, D_MODEL, P_GLA + P_RWKV_REST), D_MODEL ** -0.5),
        'mu_first': jax.random.uniform(next(ks), (P_RWKV_FIRST,), F32),
        'mu_rest': jax.random.uniform(next(ks), (DEPTH - 1, P_RWKV_REST), F32),
        'gla_w2': nrm((DEPTH, GLA_GATE_RANK, GLA_QK), GLA_GATE_RANK ** -0.5),
        'gla_b': nrm((DEPTH, GLA_QK), 0.1),
        'gla_norm': gain((DEPTH, GLA_WIDTH)),
        'rw_w2': nrm((DEPTH, RWKV_W_LORA, RWKV_WIDTH), RWKV_W_LORA ** -0.5),
        'rw_w0': nrm((DEPTH, RWKV_WIDTH), 0.5),
        'rw_a2': nrm((DEPTH, RWKV_A_LORA, RWKV_WIDTH), RWKV_A_LORA ** -0.5),
        'rw_a0': nrm((DEPTH, RWKV_WIDTH), 0.1),
        'rw_g2': nrm((DEPTH, RWKV_G_LORA, RWKV_WIDTH), RWKV_G_LORA ** -0.5),
        'rw_kk': 0.85 + nrm((DEPTH, RWKV_WIDTH), 0.02),
        'rw_ka': gain((DEPTH, RWKV_WIDTH)),
        'rw_rk': nrm((DEPTH, RWKV_HEADS, RWKV_HEAD), 0.1),
        'rw_gn_w': gain((DEPTH, RWKV_WIDTH)),
        'rw_gn_b': nrm((DEPTH, RWKV_WIDTH), 0.02),
        'rw_v2': nrm((DEPTH - 1, RWKV_V_LORA, RWKV_WIDTH), RWKV_V_LORA ** -0.5),
        'rw_v0': 1.0 + nrm((DEPTH - 1, RWKV_WIDTH), 0.1),
        'w_out': nrm((DEPTH, MIX_WIDTH, D_MODEL), MIX_WIDTH ** -0.5),
        'ffn_w_gate': nrm((n_dense, D_MODEL, FFN_DIM), D_MODEL ** -0.5),
        'ffn_w_up': nrm((n_dense, D_MODEL, FFN_DIM), D_MODEL ** -0.5),
        'ffn_w_down': nrm((n_dense, FFN_DIM, D_MODEL), FFN_DIM ** -0.5),
        'moe_router': nrm((n_moe, D_MODEL, MOE_EXPERTS), D_MODEL ** -0.5),
        'moe_w_gate': nrm((n_moe, MOE_EXPERTS, D_MODEL, MOE_FFN_DIM), D_MODEL ** -0.5),
        'moe_w_up': nrm((n_moe, MOE_EXPERTS, D_MODEL, MOE_FFN_DIM), D_MODEL ** -0.5),
        'moe_w_down': nrm((n_moe, MOE_EXPERTS, MOE_FFN_DIM, D_MODEL), MOE_FFN_DIM ** -0.5),
    }


def reference(x_prompt, x_sample, state_gla, state_rwkv, state_shift, norm_mix, norm_ffn, norm_final,
              w_in_first, w_in_rest, mu_first, mu_rest, gla_w2, gla_b, gla_norm,
              rw_w2, rw_w0, rw_a2, rw_a0, rw_g2, rw_kk, rw_ka, rw_rk, rw_gn_w, rw_gn_b, rw_v2, rw_v0,
              w_out, ffn_w_gate, ffn_w_up, ffn_w_down, moe_router, moe_w_gate, moe_w_up, moe_w_down):
    p = {
        'norm_mix': norm_mix, 'norm_ffn': norm_ffn, 'norm_final': norm_final,
        'w_in_first': w_in_first, 'w_in_rest': w_in_rest, 'mu_first': mu_first, 'mu_rest': mu_rest,
        'gla_w2': gla_w2, 'gla_b': gla_b, 'gla_norm': gla_norm,
        'rw_w2': rw_w2, 'rw_w0': rw_w0, 'rw_a2': rw_a2, 'rw_a0': rw_a0, 'rw_g2': rw_g2,
        'rw_kk': rw_kk, 'rw_ka': rw_ka, 'rw_rk': rw_rk, 'rw_gn_w': rw_gn_w, 'rw_gn_b': rw_gn_b,
        'rw_v2': rw_v2, 'rw_v0': rw_v0, 'w_out': w_out,
        'ffn_w_gate': ffn_w_gate, 'ffn_w_up': ffn_w_up, 'ffn_w_down': ffn_w_down,
        'moe_router': moe_router, 'moe_w_gate': moe_w_gate, 'moe_w_up': moe_w_up, 'moe_w_down': moe_w_down,
    }
    z_gla = jnp.zeros((DEPTH, BATCH, GLA_HEADS, GLA_DK, GLA_DV), state_gla.dtype)
    z_rwkv = jnp.zeros((DEPTH, BATCH, RWKV_HEADS, RWKV_HEAD, RWKV_HEAD), state_rwkv.dtype)
    z_shift = jnp.zeros((DEPTH, BATCH, D_MODEL), state_shift.dtype)
    y_prompt, gla_p, rwkv_p, shift_p = trunk(x_prompt, z_gla, z_rwkv, z_shift, p)
    y_sample, gla_s, rwkv_s, shift_s = trunk(x_sample, state_gla, state_rwkv, state_shift, p)
    return (y_prompt, y_sample, gla_p, gla_s, rwkv_p, rwkv_s, shift_p, shift_s)
```

```python
import functools
import math

import jax
import jax.numpy as jnp
from jax import lax
from jax.experimental import pallas as pl
from jax.experimental.pallas import tpu as pltpu

F32 = jnp.float32
BF16 = jnp.bfloat16
HIGHEST = lax.Precision.HIGHEST

NORM_EPS = 1e-6
HEAD_NORM_EPS = 1e-5
GN_EPS = 64e-5
GLA_GATE_TAU = 16.0

LANE = 128
SUBLANE = 8
RW_HEAD = 64
VMEM_CAP = 64 << 20
VMEM_BUDGET = VMEM_CAP - (6 << 20)


def _round_up(n, m):
    return (n + m - 1) // m * m


def _tile(n, pref, align):
    t = min(pref, n)
    t -= t % align
    while t >= align:
        if n % t == 0:
            return t
        t -= align
    return n


def _nbytes(shape, dtype):
    return math.prod(shape) * jnp.dtype(dtype).itemsize


def _params(sem, blocks, scratch=0):
    need = 2 * sum(blocks) + scratch
    limit = min(VMEM_BUDGET, max(need + need // 4, 16 << 20))
    return pltpu.CompilerParams(dimension_semantics=sem, vmem_limit_bytes=int(limit))


def _sigmoid(x):
    return 1.0 / (1.0 + jnp.exp(-x))


def _softplus(x):
    return jnp.maximum(x, 0.0) + jnp.log(1.0 + jnp.exp(-jnp.abs(x)))


def _dot(a, b, **kw):
    return jnp.dot(a, b, preferred_element_type=F32, **kw)


def _rmsnorm_kernel(x_ref, g_ref, o_ref):
    x = x_ref[...]
    y = x * lax.rsqrt(jnp.mean(x * x, axis=-1, keepdims=True) + NORM_EPS)
    o_ref[...] = (y * g_ref[...]).astype(o_ref.dtype)


def rmsnorm(x, g, out_dtype):
    m, d = x.shape
    tr = _tile(m, 256, SUBLANE)
    return pl.pallas_call(
        _rmsnorm_kernel,
        grid=(m // tr,),
        in_specs=[pl.BlockSpec((tr, d), lambda i: (i, 0)), pl.BlockSpec((1, d), lambda i: (0, 0))],
        out_specs=pl.BlockSpec((tr, d), lambda i: (i, 0)),
        out_shape=jax.ShapeDtypeStruct((m, d), out_dtype),
        compiler_params=_params(("arbitrary",), [_nbytes((tr, d), F32), _nbytes((tr, d), out_dtype)]),
        name="rmsnorm",
    )(x, g.reshape(1, d))


def _mm_kernel(*refs, n_a, has_res, cast_w):
    a_refs = refs[:n_a]
    w_ref = refs[n_a]
    res_ref = refs[n_a + 1] if has_res else None
    o_ref = refs[n_a + 1 + has_res]
    if cast_w:
        wb_ref = refs[n_a + 2 + has_res]

        @pl.when(pl.program_id(1) == 0)
        def _():
            wb_ref[...] = w_ref[...].astype(BF16)
        w_ref = wb_ref
    acc = None
    off = 0
    for a_ref in a_refs:
        ka = a_ref.shape[1]
        part = _dot(a_ref[...], w_ref[off:off + ka, :])
        acc = part if acc is None else acc + part
        off += ka
    if has_res:
        acc = acc + res_ref[...]
    o_ref[...] = acc.astype(o_ref.dtype)


def matmul(a_list, w, n_out, res=None, out_dtype=F32, tm_pref=1024, tn_pref=512, name="matmul"):
    m = a_list[0].shape[0]
    k = w.shape[0]
    assert sum(a.shape[1] for a in a_list) == k
    tm = _tile(m, tm_pref, 16)
    tn = _tile(n_out, tn_pref, LANE)
    cast_w = w.dtype != BF16
    in_specs = [pl.BlockSpec((tm, a.shape[1]), lambda j, i: (i, 0)) for a in a_list]
    in_specs.append(pl.BlockSpec((k, tn), lambda j, i: (0, j)))
    args = list(a_list) + [w]
    blocks = [_nbytes((tm, k), BF16), _nbytes((k, tn), w.dtype), _nbytes((tm, tn), out_dtype)]
    if res is not None:
        in_specs.append(pl.BlockSpec((tm, tn), lambda j, i: (i, j)))
        args.append(res)
        blocks.append(_nbytes((tm, tn), F32))
    scratch = [pltpu.VMEM((k, tn), BF16)] if cast_w else []
    return pl.pallas_call(
        functools.partial(_mm_kernel, n_a=len(a_list), has_res=res is not None, cast_w=cast_w),
        grid=(n_out // tn, m // tm),
        in_specs=in_specs,
        out_specs=pl.BlockSpec((tm, tn), lambda j, i: (i, j)),
        out_shape=jax.ShapeDtypeStruct((m, n_out), out_dtype),
        scratch_shapes=scratch,
        compiler_params=_params(("arbitrary", "arbitrary"), blocks,
                                _nbytes((k, tn), BF16) * cast_w + 2 * _nbytes((tm, tn), F32)),
        name=name,
    )(*args)


def _gateup_kernel(a_ref, wg_ref, wu_ref, o_ref, wgb_ref, wub_ref):
    @pl.when(pl.program_id(1) == 0)
    def _():
        wgb_ref[...] = wg_ref[...].astype(BF16)
        wub_ref[...] = wu_ref[...].astype(BF16)
    a = a_ref[...]
    g = _dot(a, wgb_ref[...])
    u = _dot(a, wub_ref[...])
    o_ref[...] = (g * _sigmoid(g) * u).astype(o_ref.dtype)


def gateup(a, wg, wu, tm_pref=1024, tn_pref=256):
    m, k = a.shape
    f = wg.shape[1]
    tm = _tile(m, tm_pref, 16)
    tn = _tile(f, tn_pref, LANE)
    blocks = [_nbytes((tm, k), BF16), 2 * _nbytes((k, tn), F32), _nbytes((tm, tn), BF16)]
    return pl.pallas_call(
        _gateup_kernel,
        grid=(f // tn, m // tm),
        in_specs=[pl.BlockSpec((tm, k), lambda j, i: (i, 0)),
                  pl.BlockSpec((k, tn), lambda j, i: (0, j)),
                  pl.BlockSpec((k, tn), lambda j, i: (0, j))],
        out_specs=pl.BlockSpec((tm, tn), lambda j, i: (i, j)),
        out_shape=jax.ShapeDtypeStruct((m, f), BF16),
        scratch_shapes=[pltpu.VMEM((k, tn), BF16), pltpu.VMEM((k, tn), BF16)],
        compiler_params=_params(("arbitrary", "arbitrary"), blocks,
                                2 * _nbytes((k, tn), BF16) + 3 * _nbytes((tm, tn), F32)),
        name="ffn_gateup",
    )(a, wg, wu)


def _block_diag_ones(n, block, dtype):
    r = lax.broadcasted_iota(jnp.int32, (n, n), 0) // block
    c = lax.broadcasted_iota(jnp.int32, (n, n), 1) // block
    return jnp.where(r == c, 1.0, 0.0).astype(dtype)


def _head_sums(x, bd):
    w = x.shape[1]
    parts = [_dot(x[:, s:s + LANE], bd, precision=HIGHEST) for s in range(0, w, LANE)]
    return parts[0] if len(parts) == 1 else jnp.concatenate(parts, axis=1)


def _rw_prep_kernel(*refs, rw, seg, has_vfirst, n_prompt, t_prompt):
    if has_vfirst:
        (p_ref, prev_ref, vf_ref, mu_ref, w2_ref, w0_ref, a2_ref, a0_ref, g2_ref, kk_ref, ka_ref,
         v2_ref, v0_ref, r_out, w_out, k_out, v_out, a_out, b_out, g_out) = refs
    else:
        (p_ref, prev_ref, mu_ref, w2_ref, w0_ref, a2_ref, a0_ref, g2_ref, kk_ref, ka_ref,
         r_out, w_out, k_out, v_out, a_out, b_out, g_out) = refs
    tr = p_ref.shape[0]
    row0 = pl.program_id(0) * tr
    rows = lax.broadcasted_iota(jnp.int32, (tr, 1), 0)
    grow = rows + row0
    seq_start = jnp.logical_and(grow < n_prompt, grow % t_prompt == 0)

    def shifted_lerp(lo, width):
        p = p_ref[:, lo:lo + width]
        above = pltpu.roll(p, 1, axis=0)
        above = jnp.where(rows == 0, prev_ref[SUBLANE - 1:SUBLANE, lo:lo + width], above)
        above = jnp.where(seq_start, 0.0, above)
        return p + mu_ref[:, lo:lo + width] * (above - p)

    o_r, o_k, o_v, o_w, w_w, o_a, w_a, o_g, w_g, o_xv, w_xv = seg
    r = shifted_lerp(o_r, rw)
    kr = shifted_lerp(o_k, rw)
    vr = shifted_lerp(o_v, rw)
    xw = shifted_lerp(o_w, w_w)
    xa = shifted_lerp(o_a, w_a)
    xg = shifted_lerp(o_g, w_g)

    w_log = -_softplus(-(w0_ref[...] + _dot(jnp.tanh(xw).astype(BF16), w2_ref[...]))) - 0.5
    decay = jnp.exp(-jnp.exp(w_log))
    a = _sigmoid(a0_ref[...] + _dot(xa.astype(BF16), a2_ref[...]))
    g = _dot(_sigmoid(xg).astype(BF16), g2_ref[...])
    if has_vfirst:
        xv = shifted_lerp(o_xv, w_xv)
        vr = vr + (vf_ref[...] - vr) * _sigmoid(v0_ref[...] + _dot(xv.astype(BF16), v2_ref[...]))
    kk = kr * kk_ref[...]
    bd = _block_diag_ones(LANE, RW_HEAD, F32)
    kk = kk * lax.rsqrt(jnp.maximum(_head_sums(kk * kk, bd), 1e-24))
    r_out[...] = r
    w_out[...] = decay
    k_out[...] = kr * (1.0 + (a - 1.0) * ka_ref[...])
    v_out[...] = vr
    a_out[...] = -kk
    b_out[...] = kk * a
    g_out[...] = g


def rw_prep(p_rw, seg, rw, mu, w2, w0, a2, a0, g2, kk, ka, v2, v0, v_first, n_prompt, t_prompt):
    m, n_rw = p_rw.shape
    tr = _tile(m, 128, SUBLANE)
    nsub = tr // SUBLANE
    has_vfirst = v_first is not None
    row = lambda w: pl.BlockSpec((1, w), lambda i: (0, 0))
    full = lambda x: pl.BlockSpec(x.shape, lambda i: (0, 0))
    tile = pl.BlockSpec((tr, rw), lambda i: (i, 0))
    in_specs = [pl.BlockSpec((tr, n_rw), lambda i: (i, 0)),
                pl.BlockSpec((SUBLANE, n_rw), lambda i: (jnp.maximum(i * nsub - 1, 0), 0))]
    args = [p_rw, p_rw]
    if has_vfirst:
        in_specs.append(tile)
        args.append(v_first)
    in_specs += [row(n_rw), full(w2), row(rw), full(a2), row(rw), full(g2), row(rw), row(rw)]
    args += [mu.reshape(1, n_rw), w2, w0.reshape(1, rw), a2, a0.reshape(1, rw), g2,
             kk.reshape(1, rw), ka.reshape(1, rw)]
    if has_vfirst:
        in_specs += [full(v2), row(rw)]
        args += [v2, v0.reshape(1, rw)]
    blocks = [_nbytes((tr + SUBLANE, n_rw), F32), (7 + has_vfirst) * _nbytes((tr, rw), F32),
              sum(_nbytes(x.shape, x.dtype) for x in (w2, a2, g2)) + (_nbytes(v2.shape, v2.dtype) if has_vfirst else 0)]
    outs = pl.pallas_call(
        functools.partial(_rw_prep_kernel, rw=rw, seg=seg, has_vfirst=has_vfirst,
                          n_prompt=n_prompt, t_prompt=t_prompt),
        grid=(m // tr,),
        in_specs=in_specs,
        out_specs=[tile] * 7,
        out_shape=[jax.ShapeDtypeStruct((m, rw), F32)] * 7,
        compiler_params=_params(("arbitrary",), blocks, 12 * _nbytes((tr, rw), F32)),
        name="rwkv_prep",
    )(*args)
    return outs


def _rw_scan_kernel(r_ref, w_ref, k_ref, v_ref, a_ref, b_ref, g_ref, gnw_ref, gnb_ref, rk_ref, s0_ref,
                    o_ref, sout_ref, s_scr, y_scr, *, n_pairs, t_start):
    tc = r_ref.shape[0]
    c = pl.program_id(1)

    @pl.when(c == 0)
    def _():
        for pair in range(n_pairs):
            s_scr[pair * RW_HEAD:(pair + 1) * RW_HEAD, :] = jnp.concatenate(
                [s0_ref[0, 2 * pair], s0_ref[0, 2 * pair + 1]], axis=1)

    bd = _block_diag_ones(LANE, RW_HEAD, BF16)
    diag = (lax.broadcasted_iota(jnp.int32, (RW_HEAD, LANE), 1) % RW_HEAD
            == lax.broadcasted_iota(jnp.int32, (RW_HEAD, LANE), 0))

    sub = lax.broadcasted_iota(jnp.int32, (SUBLANE, LANE), 0)

    def group_step(base, first):
        for pair in range(n_pairs):
            lanes = slice(pair * LANE, (pair + 1) * LANE)
            rows = slice(pair * RW_HEAD, (pair + 1) * RW_HEAD)
            r8, w8, k8, v8, a8, b8 = (ref[pl.ds(base, SUBLANE), lanes]
                                      for ref in (r_ref, w_ref, k_ref, v_ref, a_ref, b_ref))
            s = s_scr[rows, :]
            y8 = jnp.zeros((SUBLANE, LANE), F32)
            for j in range(first, SUBLANE):
                row = lambda x8: x8[j:j + 1, :]
                sa_in = s * row(a8)
                v_in = jnp.where(diag, row(v8), 0.0)
                bc = _dot(jnp.concatenate([sa_in, v_in], axis=0).astype(BF16), bd)
                s = s * row(w8) + bc[:RW_HEAD] * row(b8) + bc[RW_HEAD:] * row(k8)
                y_bc = _dot((s * row(r8)).astype(BF16), bd)
                y_row = jnp.sum(jnp.where(diag, y_bc, 0.0), axis=0, keepdims=True)
                y8 = jnp.where(sub == j, y_row, y8)
            s_scr[rows, :] = s
            y_scr[pl.ds(base, SUBLANE), lanes] = y8

    if tc == SUBLANE:
        group_step(0, t_start)
    else:
        assert t_start == 0 and tc % SUBLANE == 0

        def body(i, carry):
            group_step(pl.multiple_of(i * SUBLANE, SUBLANE), 0)
            return carry

        lax.fori_loop(0, tc // SUBLANE, body, 0)

    bdf = _block_diag_ones(LANE, RW_HEAD, F32)
    y = y_scr[...]
    mean = _head_sums(y, bdf) * (1.0 / RW_HEAD)
    yc = y - mean
    var = _head_sums(yc * yc, bdf) * (1.0 / RW_HEAD)
    yn = yc * lax.rsqrt(var + GN_EPS) * gnw_ref[...] + gnb_ref[...]
    v = v_ref[...]
    bonus = _head_sums(r_ref[...] * k_ref[...] * rk_ref[...], bdf) * v
    o_ref[...] = ((yn + bonus) * g_ref[...]).astype(o_ref.dtype)

    @pl.when(c == pl.num_programs(1) - 1)
    def _():
        for pair in range(n_pairs):
            s = s_scr[pair * RW_HEAD:(pair + 1) * RW_HEAD, :]
            sout_ref[0, 2 * pair] = s[:, :RW_HEAD]
            sout_ref[0, 2 * pair + 1] = s[:, RW_HEAD:]


def rw_scan(ops, g, gn_w, gn_b, rk, s0, row0, n_seq, t_seq, tc, t_start):
    rw = ops[0].shape[1]
    n_heads = s0.shape[1]
    n_chunks = t_seq // tc
    base = row0 // tc
    tile = pl.BlockSpec((tc, rw), lambda i, c: (base + i * n_chunks + c, 0))
    row = pl.BlockSpec((1, rw), lambda i, c: (0, 0))
    st = pl.BlockSpec((1, n_heads, RW_HEAD, RW_HEAD), lambda i, c: (i, 0, 0, 0))
    blocks = [7 * _nbytes((tc, rw), F32), _nbytes((tc, rw), BF16), 4 * _nbytes((n_heads, RW_HEAD, LANE), F32)]
    o, s_out = pl.pallas_call(
        functools.partial(_rw_scan_kernel, n_pairs=n_heads // 2, t_start=t_start),
        grid=(n_seq, n_chunks),
        in_specs=[tile] * 7 + [row] * 3 + [st],
        out_specs=[pl.BlockSpec((tc, rw), lambda i, c: (i * n_chunks + c, 0)), st],
        out_shape=[jax.ShapeDtypeStruct((n_seq * t_seq, rw), BF16),
                   jax.ShapeDtypeStruct((n_seq, n_heads, RW_HEAD, RW_HEAD), F32)],
        scratch_shapes=[pltpu.VMEM((n_heads // 2 * RW_HEAD, LANE), F32), pltpu.VMEM((tc, rw), F32)],
        compiler_params=_params(("arbitrary", "arbitrary"), blocks, 8 * _nbytes((tc, rw), F32)),
        name="rwkv_scan",
    )(*ops, g, gn_w.reshape(1, rw), gn_b.reshape(1, rw), rk.reshape(1, rw), s0)
    return o, s_out


def _gla_kernel(q_ref, k_ref, v_ref, go_ref, lr_ref, w2_ref, gb_ref, gn_ref, s0_ref, o_ref, sout_ref, s_scr,
                *, n_heads, dk, dv, t_start):
    c_rows = q_ref.shape[0]
    n = pl.program_id(1)

    @pl.when(n == 0)
    def _():
        s_scr[...] = s0_ref[0]

    rows = lax.broadcasted_iota(jnp.int32, (c_rows, 1), 0)
    valid = rows >= t_start
    tri = (lax.broadcasted_iota(jnp.int32, (c_rows, c_rows), 0)
           >= lax.broadcasted_iota(jnp.int32, (c_rows, c_rows), 1))
    tri_f = jnp.where(tri, 1.0, 0.0).astype(F32)
    lr = lr_ref[...]
    mid = max(c_rows // 2 - 1, 0)
    scale = dk ** -0.5
    for h in range(n_heads):
        ks = slice(h * dk, (h + 1) * dk)
        vs = slice(h * dv, (h + 1) * dv)
        x = _dot(lr, w2_ref[:, ks], precision=HIGHEST) + gb_ref[:, ks]
        log_a = (jnp.minimum(x, 0.0) - jnp.log(1.0 + jnp.exp(-jnp.abs(x)))) * (1.0 / GLA_GATE_TAU)
        log_a = jnp.where(valid, log_a, 0.0)
        b = _dot(tri_f, log_a, precision=HIGHEST)
        b_last = b[c_rows - 1:c_rows, :]
        b_mid = b[mid:mid + 1, :]
        q = q_ref[:, ks] * scale
        k = jnp.where(valid, k_ref[:, ks], 0.0)
        v = v_ref[:, vs].astype(BF16)
        qi = (q * jnp.exp(b - b_mid)).astype(BF16)
        ki = (k * jnp.exp(b_mid - b)).astype(BF16)
        att = lax.dot_general(qi, ki, (((1,), (1,)), ((), ())), preferred_element_type=F32)
        att = jnp.where(tri, att, 0.0).astype(BF16)
        s = s_scr[h]
        o = _dot(att, v) + _dot((q * jnp.exp(b)).astype(BF16), s.astype(BF16))
        kd = (k * jnp.exp(b_last - b)).astype(BF16)
        upd = lax.dot_general(kd, v, (((0,), (0,)), ((), ())), preferred_element_type=F32)
        dcol = jnp.transpose(jnp.broadcast_to(jnp.exp(b_last), (LANE, dk)))
        s_scr[h] = s * jnp.concatenate([dcol] * (dv // LANE), axis=1) + upd
        on = o * lax.rsqrt(jnp.mean(o * o, axis=-1, keepdims=True) + HEAD_NORM_EPS)
        go = go_ref[:, vs]
        o_ref[:, vs] = (on * gn_ref[:, vs] * (go * _sigmoid(go))).astype(o_ref.dtype)

    @pl.when(n == pl.num_programs(1) - 1)
    def _():
        sout_ref[0] = s_scr[...]


def gla(p_gla, p_rw, lr_block, w2p, gb, gn, s0, row0, n_seq, t_seq, c_rows, t_start):
    _, n_heads, dk, dv = s0.shape
    qk, gw = n_heads * dk, n_heads * dv
    n_chunks = t_seq // c_rows
    base = row0 // c_rows
    rmap = lambda col: (lambda i, n: (base + i * n_chunks + n, col))
    st = pl.BlockSpec((1, n_heads, dk, dv), lambda i, n: (i, 0, 0, 0))
    const = lambda shape: pl.BlockSpec(shape, lambda i, n: (0, 0))
    blocks = [_nbytes((c_rows, 2 * qk + 2 * gw + LANE), F32), _nbytes((LANE + 2, qk), F32),
              _nbytes((c_rows, gw), BF16), 2 * _nbytes((n_heads, dk, dv), F32)]
    o, s_out = pl.pallas_call(
        functools.partial(_gla_kernel, n_heads=n_heads, dk=dk, dv=dv, t_start=t_start),
        grid=(n_seq, n_chunks),
        in_specs=[pl.BlockSpec((c_rows, qk), rmap(0)),
                  pl.BlockSpec((c_rows, qk), rmap(1)),
                  pl.BlockSpec((c_rows, gw), rmap(2 * qk // gw)),
                  pl.BlockSpec((c_rows, gw), rmap(2 * qk // gw + 1)),
                  pl.BlockSpec((c_rows, LANE), rmap(lr_block)),
                  const((LANE, qk)), const((1, qk)), const((1, gw)), st],
        out_specs=[pl.BlockSpec((c_rows, gw), lambda i, n: (i * n_chunks + n, 0)), st],
        out_shape=[jax.ShapeDtypeStruct((n_seq * t_seq, gw), BF16),
                   jax.ShapeDtypeStruct((n_seq, n_heads, dk, dv), F32)],
        scratch_shapes=[pltpu.VMEM((n_heads, dk, dv), F32)],
        compiler_params=_params(("arbitrary", "arbitrary"), blocks,
                                _nbytes((n_heads, dk, dv), F32) + 6 * _nbytes((dk, dv), F32)),
        name="gla",
    )(p_gla, p_gla, p_gla, p_gla, p_rw, w2p, gb.reshape(1, qk), gn.reshape(1, gw), s0)
    return o, s_out


def _router_kernel(x_ref, g_ref, wr_ref, h_ref, info_ref, *, n_experts):
    x = x_ref[...]
    h = x * lax.rsqrt(jnp.mean(x * x, axis=-1, keepdims=True) + NORM_EPS) * g_ref[...]
    half = h.shape[1] // 2
    lo = lax.bitcast_convert_type(h[:, :half].astype(BF16).astype(F32), jnp.uint32)
    hi = lax.bitcast_convert_type(h[:, half:].astype(BF16).astype(F32), jnp.uint32)
    h_ref[...] = (lo >> 16) | (hi & jnp.uint32(0xFFFF0000))
    logits = _dot(h, wr_ref[...], precision=HIGHEST)
    lane = lax.broadcasted_iota(jnp.int32, logits.shape, 1)
    neg = jnp.float32(-jnp.inf)
    logits = jnp.where(lane < n_experts, logits, neg)
    m1 = jnp.max(logits, axis=-1, keepdims=True)
    i1 = jnp.min(jnp.where(logits == m1, lane, LANE), axis=-1, keepdims=True)
    rest = jnp.where(lane == i1, neg, logits)
    m2 = jnp.max(rest, axis=-1, keepdims=True)
    i2 = jnp.min(jnp.where(rest == m2, lane, LANE), axis=-1, keepdims=True)
    e2 = jnp.exp(m2 - m1)
    g1 = 1.0 / (1.0 + e2)
    g2 = e2 / (1.0 + e2)
    info = jnp.where(lane == 0, i1.astype(F32),
                     jnp.where(lane == 1, i2.astype(F32),
                               jnp.where(lane == 2, g1, jnp.where(lane == 3, g2, 0.0))))
    info_ref[...] = info


def moe_router(x, g, router):
    m, d = x.shape
    n_experts = router.shape[1]
    tr = _tile(m, 256, SUBLANE)
    wr = jnp.zeros((d, LANE), F32).at[:, :n_experts].set(router)
    blocks = [_nbytes((tr, d), F32), _nbytes((d, LANE), F32), _nbytes((tr, d // 2), jnp.uint32)]
    return pl.pallas_call(
        functools.partial(_router_kernel, n_experts=n_experts),
        grid=(m // tr,),
        in_specs=[pl.BlockSpec((tr, d), lambda i: (i, 0)), pl.BlockSpec((1, d), lambda i: (0, 0)),
                  pl.BlockSpec((d, LANE), lambda i: (0, 0))],
        out_specs=[pl.BlockSpec((tr, d // 2), lambda i: (i, 0)), pl.BlockSpec((tr, LANE), lambda i: (i, 0))],
        out_shape=[jax.ShapeDtypeStruct((m, d // 2), jnp.uint32), jax.ShapeDtypeStruct((m, LANE), F32)],
        compiler_params=_params(("arbitrary",), blocks, 4 * _nbytes((tr, d), F32)),
        name="moe_router",
    )(x, g.reshape(1, d), wr)


GATHER_WINDOW = 32


def _gather_kernel(idx_ref, src_ref, out_ref, sem):
    tg = idx_ref.shape[2]
    base = pl.program_id(0) * tg

    def copy(r):
        return pltpu.make_async_copy(src_ref.at[pl.ds(idx_ref[0, 0, r], 1)], out_ref.at[pl.ds(base + r, 1)], sem)

    def body(r, carry):
        copy(r).start()

        @pl.when(r >= GATHER_WINDOW)
        def _():
            copy(r - GATHER_WINDOW).wait()
        return carry

    lax.fori_loop(0, tg, body, 0)

    def drain(r, carry):
        copy(r).wait()
        return carry

    lax.fori_loop(max(tg - GATHER_WINDOW, 0), tg, drain, 0)


def gather_rows(src, idx, tg=256):
    n = idx.shape[0]
    tg = _tile(n, tg, SUBLANE)
    return pl.pallas_call(
        _gather_kernel,
        grid=(n // tg,),
        in_specs=[pl.BlockSpec((1, 1, tg), lambda i: (i, 0, 0), memory_space=pltpu.SMEM),
                  pl.BlockSpec(memory_space=pl.ANY)],
        out_specs=pl.BlockSpec(memory_space=pl.ANY),
        out_shape=jax.ShapeDtypeStruct((n, src.shape[1]), src.dtype),
        scratch_shapes=[pltpu.SemaphoreType.DMA(())],
        compiler_params=pltpu.CompilerParams(dimension_semantics=("arbitrary",)),
        name="moe_gather",
    )(idx.reshape(n // tg, 1, tg), src)


def _unpack_bf16(words):
    lo = lax.bitcast_convert_type(words << 16, F32).astype(BF16)
    hi = lax.bitcast_convert_type(words & jnp.uint32(0xFFFF0000), F32).astype(BF16)
    return lo, hi


def _moe_gateup_kernel(te_ref, tv_ref, a_ref, wg_ref, wu_ref, o_ref, *, kc):
    t = pl.program_id(1)

    @pl.when(tv_ref[t] == 1)
    def _():
        halves = _unpack_bf16(a_ref[...])
        half = halves[0].shape[1]
        g = u = None
        for part, a in enumerate(halves):
            for lo in range(0, half, kc):
                rows = slice(part * half + lo, part * half + lo + kc)
                pg = _dot(a[:, lo:lo + kc], wg_ref[rows, :].astype(BF16))
                pu = _dot(a[:, lo:lo + kc], wu_ref[rows, :].astype(BF16))
                g = pg if g is None else g + pg
                u = pu if u is None else u + pu
        o_ref[...] = (g * _sigmoid(g) * u).astype(o_ref.dtype)

    @pl.when(tv_ref[t] == 0)
    def _():
        o_ref[...] = jnp.zeros(o_ref.shape, o_ref.dtype)


def moe_gateup(xs, wg, wu, tile_expert, tile_valid, tm, tn_pref=512):
    r = xs.shape[0]
    _, d, f = wg.shape
    tn = _tile(f, tn_pref, LANE)
    kc = _tile(d // 2, 512, LANE)
    wspec = pl.BlockSpec((None, d, tn), lambda j, t, te, tv: (te[t], 0, j))
    blocks = [_nbytes((tm, d), BF16), 2 * _nbytes((d, tn), F32), _nbytes((tm, tn), BF16)]
    return pl.pallas_call(
        functools.partial(_moe_gateup_kernel, kc=kc),
        grid_spec=pltpu.PrefetchScalarGridSpec(
            num_scalar_prefetch=2,
            grid=(f // tn, r // tm),
            in_specs=[pl.BlockSpec((tm, d // 2), lambda j, t, te, tv: (t, 0)), wspec, wspec],
            out_specs=pl.BlockSpec((tm, tn), lambda j, t, te, tv: (t, j))),
        out_shape=jax.ShapeDtypeStruct((r, f), BF16),
        compiler_params=_params(("arbitrary", "arbitrary"), blocks,
                                _nbytes((tm, d), BF16) + 4 * _nbytes((kc, tn), BF16) + 3 * _nbytes((tm, tn), F32)),
        name="moe_gateup",
    )(tile_expert, tile_valid, xs, wg, wu)


def _moe_down_kernel(te_ref, tf_ref, tv_ref, a_ref, w_ref, o_ref, wb_ref):
    t = pl.program_id(1)
    k = pl.program_id(2)

    @pl.when(tf_ref[t] == 1)
    def _():
        wb_ref[k] = w_ref[...].astype(BF16)

    @pl.when(k == 0)
    def _():
        o_ref[...] = jnp.zeros(o_ref.shape, o_ref.dtype)

    @pl.when(tv_ref[t] == 1)
    def _():
        o_ref[...] += _dot(a_ref[...], wb_ref[k])


def moe_down(act, wd, tile_expert, tile_first, tile_valid, tm, tn_pref=512, tk_pref=3584):
    r, f = act.shape
    d = wd.shape[2]
    tn = _tile(d, tn_pref, LANE)
    tk = _tile(f, tk_pref, LANE)
    nk = f // tk

    def wmap(j, t, k, te, tf, tv):
        return (te[t], jnp.where(tf[t] == 1, k, nk - 1), j)

    blocks = [_nbytes((tm, tk), BF16), _nbytes((tk, tn), F32), _nbytes((tm, tn), F32)]
    return pl.pallas_call(
        _moe_down_kernel,
        grid_spec=pltpu.PrefetchScalarGridSpec(
            num_scalar_prefetch=3,
            grid=(d // tn, r // tm, nk),
            in_specs=[pl.BlockSpec((tm, tk), lambda j, t, k, te, tf, tv: (t, k)),
                      pl.BlockSpec((None, tk, tn), wmap)],
            out_specs=pl.BlockSpec((tm, tn), lambda j, t, k, te, tf, tv: (t, j)),
            scratch_shapes=[pltpu.VMEM((nk, tk, tn), BF16)]),
        out_shape=jax.ShapeDtypeStruct((r, d), F32),
        compiler_params=_params(("arbitrary", "arbitrary", "arbitrary"), blocks,
                                _nbytes((f, tn), BF16) + _nbytes((tm, tn), F32)),
        name="moe_down",
    )(tile_expert, tile_first, tile_valid, act, wd)


def _combine_kernel(p1_ref, p2_ref, x_ref, info_ref, ys_ref, o_ref, buf1, buf2, sem):
    tc = x_ref.shape[0]

    def copies(r):
        return (pltpu.make_async_copy(ys_ref.at[pl.ds(p1_ref[0, 0, r], 1)], buf1.at[pl.ds(r, 1)], sem.at[0]),
                pltpu.make_async_copy(ys_ref.at[pl.ds(p2_ref[0, 0, r], 1)], buf2.at[pl.ds(r, 1)], sem.at[1]))

    def start(r, carry):
        c1, c2 = copies(r)
        c1.start()
        c2.start()
        return carry

    def wait(r, carry):
        c1, c2 = copies(r)
        c1.wait()
        c2.wait()
        return carry

    lax.fori_loop(0, tc, start, 0)
    lax.fori_loop(0, tc, wait, 0)
    info = info_ref[...]
    o_ref[...] = x_ref[...] + info[:, 2:3] * buf1[...] + info[:, 3:4] * buf2[...]


def moe_combine(x, info, ys, pos1, pos2, tc=128):
    m, d = x.shape
    tc = _tile(m, tc, SUBLANE)
    idx = pl.BlockSpec((1, 1, tc), lambda i: (i, 0, 0), memory_space=pltpu.SMEM)
    blocks = [2 * _nbytes((tc, d), F32), _nbytes((tc, LANE), F32)]
    return pl.pallas_call(
        _combine_kernel,
        grid=(m // tc,),
        in_specs=[idx, idx, pl.BlockSpec((tc, d), lambda i: (i, 0)), pl.BlockSpec((tc, LANE), lambda i: (i, 0)),
                  pl.BlockSpec(memory_space=pl.ANY)],
        out_specs=pl.BlockSpec((tc, d), lambda i: (i, 0)),
        out_shape=jax.ShapeDtypeStruct((m, d), F32),
        scratch_shapes=[pltpu.VMEM((tc, d), F32), pltpu.VMEM((tc, d), F32), pltpu.SemaphoreType.DMA((2,))],
        compiler_params=_params(("arbitrary",), blocks, 4 * _nbytes((tc, d), F32)),
        name="moe_combine",
    )(pos1.reshape(m // tc, 1, tc), pos2.reshape(m // tc, 1, tc), x, info, ys)


def moe_ffn(x, norm_g, router, wg, wu, wd, row_valid, n_valid, tm=512):
    m, d = x.shape
    n_experts = router.shape[1]
    h_packed, info = moe_router(x, norm_g, router)

    none = n_experts
    e1 = jnp.where(row_valid, info[:, 0].astype(jnp.int32), none)
    e2 = jnp.where(row_valid, info[:, 1].astype(jnp.int32), none)
    e_all = jnp.concatenate([e1, e2])
    onehot = (e_all[:, None] == jnp.arange(n_experts, dtype=jnp.int32)[None, :]).astype(jnp.int32)
    counts = jnp.sum(onehot, axis=0)
    rank = jnp.sum((jnp.cumsum(onehot, axis=0) - onehot) * onehot, axis=1)
    padded = (counts + tm - 1) // tm * tm
    ends = jnp.cumsum(padded)
    starts = ends - padded
    n_rows = _round_up(2 * n_valid + n_experts * (tm - 1), tm)
    n_tiles = n_rows // tm
    pos = jnp.where(e_all < none, jnp.take(starts, jnp.minimum(e_all, none - 1)) + rank, n_rows)
    tok = jnp.tile(jnp.arange(m, dtype=jnp.int32), 2)
    row_src = jnp.zeros((n_rows,), jnp.int32).at[pos].set(tok, mode="drop")
    tile_start = jnp.arange(n_tiles, dtype=jnp.int32) * tm
    tile_valid = (tile_start < ends[-1]).astype(jnp.int32)
    last_tile = jnp.maximum(ends[-1] // tm - 1, 0)
    tile_expert = jnp.searchsorted(ends, jnp.minimum(tile_start, last_tile * tm), side="right").astype(jnp.int32)
    tile_expert = jnp.minimum(tile_expert, n_experts - 1)
    prev_expert = jnp.concatenate([jnp.full((1,), -1, jnp.int32), tile_expert[:-1]])
    tile_first = jnp.logical_and(tile_expert != prev_expert, tile_valid == 1).astype(jnp.int32)
    tile_first = tile_first.at[0].set(1)
    pos_safe = jnp.where(pos < n_rows, pos, 0)
    gates = jnp.where(row_valid[:, None], info, 0.0)

    xs = gather_rows(h_packed, row_src)
    act = moe_gateup(xs, wg, wu, tile_expert, tile_valid, tm)
    ys = moe_down(act, wd, tile_expert, tile_first, tile_valid, tm)
    return moe_combine(x, gates, ys, pos_safe[:m], pos_safe[m:])


def _pad_cols(w, width):
    return w if w.shape[-1] == width else jnp.pad(w, [(0, 0)] * (w.ndim - 1) + [(0, width - w.shape[-1])])


def _pad_rows(w, rows):
    return w if w.shape[0] == rows else jnp.pad(w, [(0, rows - w.shape[0])] + [(0, 0)] * (w.ndim - 1))


def kernel(x_prompt, x_sample, state_gla, state_rwkv, state_shift, norm_mix, norm_ffn, norm_final,
           w_in_first, w_in_rest, mu_first, mu_rest, gla_w2, gla_b, gla_norm,
           rw_w2, rw_w0, rw_a2, rw_a0, rw_g2, rw_kk, rw_ka, rw_rk, rw_gn_w, rw_gn_b, rw_v2, rw_v0,
           w_out, ffn_w_gate, ffn_w_up, ffn_w_down, moe_router_w, moe_w_gate, moe_w_up, moe_w_down):
    batch, seq, d = x_prompt.shape
    dec_batch, dec_seq, _ = x_sample.shape
    depth = norm_mix.shape[0]
    _, _, gla_heads, dk, dv = state_gla.shape
    rw_heads = state_rwkv.shape[2]
    assert state_rwkv.shape[3] == RW_HEAD and state_rwkv.shape[4] == RW_HEAD and rw_heads % 2 == 0
    qk, gw, rw = gla_heads * dk, gla_heads * dv, rw_heads * RW_HEAD
    rank = gla_w2.shape[1]
    p_gla = 2 * qk + 2 * gw + rank
    assert dk % LANE == 0 and dv % LANE == 0 and (2 * qk) % gw == 0 and rank <= LANE

    group = _round_up(dec_seq + 1, SUBLANE)
    t_start = group - dec_seq
    n_prompt = batch * seq
    m = n_prompt + dec_batch * group
    prompt_chunk = _tile(seq, 64, SUBLANE)
    scan_chunk = _tile(seq, 128, SUBLANE)
    sample_rows = jnp.arange(dec_batch * group, dtype=jnp.int32) % group >= t_start
    row_valid = jnp.concatenate([jnp.ones((n_prompt,), bool), sample_rows])

    x = jnp.concatenate([
        x_prompt.reshape(n_prompt, d),
        jnp.concatenate([jnp.zeros((dec_batch, t_start, d), F32), x_sample], axis=1).reshape(dec_batch * group, d)])

    lw, la, lg = rw_w2.shape[1], rw_a2.shape[1], rw_g2.shape[1]
    lv = rw_v2.shape[1]
    pw, pa, pg, pv = (_round_up(n, LANE) for n in (lw, la, lg, lv))
    o_w = 3 * rw
    o_a = o_w + pw
    o_g = o_a + pa
    o_xv = o_g + pg
    o_lr = o_xv + pv
    n_rw = _round_up(o_lr + LANE, 512)
    seg = (0, rw, 2 * rw, o_w, pw, o_a, pa, o_g, pg, o_xv, pv)

    def repack(w_in_l, mu_l, has_v):
        wr = w_in_l[:, p_gla:]
        mu_l = mu_l
        parts_w, parts_mu = [], []
        src = 0
        for width, padded in ((3 * rw, 3 * rw), (lw, pw), (la, pa), (lg, pg)) + (((lv, pv),) if has_v else ()):
            parts_w.append(_pad_cols(wr[:, src:src + width], padded))
            parts_mu.append(_pad_cols(mu_l[src:src + width], padded))
            src += width
        if not has_v:
            parts_w.append(jnp.zeros((d, pv), F32))
            parts_mu.append(jnp.zeros((pv,), F32))
        parts_w.append(_pad_cols(w_in_l[:, p_gla - rank:p_gla], n_rw - o_lr))
        parts_mu.append(jnp.zeros((n_rw - o_lr,), F32))
        return jnp.concatenate(parts_w, axis=1).astype(BF16), jnp.concatenate(parts_mu)

    zeros_gla = jnp.zeros((batch, gla_heads, dk, dv), F32)
    zeros_rw = jnp.zeros((batch, rw_heads, RW_HEAD, RW_HEAD), F32)
    last_rows = jnp.concatenate([jnp.arange(batch, dtype=jnp.int32) * seq + seq - 1,
                                 n_prompt + jnp.arange(dec_batch, dtype=jnp.int32) * group + group - 1])
    n_last = _round_up(batch + dec_batch, SUBLANE)
    prev_rows = n_prompt + jnp.arange(dec_batch, dtype=jnp.int32) * group + t_start - 1

    new_gla_p, new_gla_s, new_rw_p, new_rw_s, new_shift = [], [], [], [], []
    v_first = None
    for l in range(depth):
        h = rmsnorm(x, norm_mix[l], BF16).at[prev_rows].set(state_shift[l].astype(BF16))
        x_last = _pad_rows(jnp.take(x, last_rows, axis=0), n_last)
        new_shift.append(rmsnorm(x_last, norm_mix[l], F32)[:batch + dec_batch])

        if l == 0:
            w_in_l, mu_l = w_in_first, mu_first
        else:
            w_in_l, mu_l = w_in_rest[l - 1], mu_rest[l - 1]
        w_rw, mu_p = repack(w_in_l, mu_l, l > 0)
        pg_all = matmul([h], w_in_l, 2 * qk + 2 * gw, name="w_in_gla")
        pr_all = matmul([h], w_rw, n_rw, name="w_in_rwkv")

        ops = rw_prep(pr_all, seg, rw, mu_p,
                      _pad_rows(rw_w2[l], pw).astype(BF16), rw_w0[l], _pad_rows(rw_a2[l], pa).astype(BF16), rw_a0[l],
                      _pad_rows(rw_g2[l], pg).astype(BF16), rw_kk[l], rw_ka[l],
                      _pad_rows(rw_v2[l - 1], pv).astype(BF16) if l > 0 else None,
                      rw_v0[l - 1] if l > 0 else None, v_first, n_prompt, seq)
        r_, w_, k_, v_, a_, b_, g_ = ops
        if l == 0:
            v_first = v_
        scan_args = ((r_, w_, k_, v_, a_, b_), g_, rw_gn_w[l], rw_gn_b[l], rw_rk[l].reshape(rw))
        o_rw_p, s_rw_p = rw_scan(*scan_args, zeros_rw, 0, batch, seq, scan_chunk, 0)
        o_rw_s, s_rw_s = rw_scan(*scan_args, state_rwkv[l], n_prompt, dec_batch, group, group, t_start)

        w2p = _pad_rows(gla_w2[l], LANE)
        gla_args = (pg_all, pr_all, o_lr // LANE, w2p, gla_b[l], gla_norm[l])
        o_gla_p, s_gla_p = gla(*gla_args, zeros_gla, 0, batch, seq, prompt_chunk, 0)
        o_gla_s, s_gla_s = gla(*gla_args, state_gla[l], n_prompt, dec_batch, group, group, t_start)

        o_gla = jnp.concatenate([o_gla_p, o_gla_s])
        o_rw = jnp.concatenate([o_rw_p, o_rw_s])
        x = matmul([o_gla, o_rw], w_out[l], d, res=x, name="w_out")
        new_gla_p.append(s_gla_p)
        new_gla_s.append(s_gla_s)
        new_rw_p.append(s_rw_p)
        new_rw_s.append(s_rw_s)

        i = l // 2
        if l % 2 == 0:
            h2 = rmsnorm(x, norm_ffn[l], BF16)
            act = gateup(h2, ffn_w_gate[i], ffn_w_up[i])
            x = matmul([act], ffn_w_down[i].astype(BF16), d, res=x, tm_pref=512, name="ffn_down")
        else:
            x = moe_ffn(x, norm_ffn[l], moe_router_w[i], moe_w_gate[i], moe_w_up[i], moe_w_down[i],
                        row_valid, n_prompt + dec_batch * dec_seq)

    y = rmsnorm(x, norm_final, F32)
    y_prompt = y[:n_prompt].reshape(batch, seq, d)
    y_sample = y[n_prompt:].reshape(dec_batch, group, d)[:, t_start:]
    shift = jnp.stack(new_shift)
    return (y_prompt, y_sample, jnp.stack(new_gla_p), jnp.stack(new_gla_s), jnp.stack(new_rw_p),
            jnp.stack(new_rw_s), shift[:, :batch], shift[:, batch:])
```

```python
import functools
import math

import jax
import jax.numpy as jnp
from jax import lax
from jax.experimental import pallas as pl
from jax.experimental.pallas import tpu as pltpu

F32 = jnp.float32
BF16 = jnp.bfloat16
HIGHEST = lax.Precision.HIGHEST

NORM_EPS = 1e-6
HEAD_NORM_EPS = 1e-5
GN_EPS = 64e-5
GLA_GATE_TAU = 16.0

LANE = 128
SUBLANE = 8
RW_HEAD = 64
MXU_DIM = 256
UNIT_HEADS = MXU_DIM // RW_HEAD
UNIT_LANES = MXU_DIM
VMEM_CAP = 64 << 20
VMEM_BUDGET = VMEM_CAP - (6 << 20)


def _round_up(n, m):
    return (n + m - 1) // m * m


def _tile(n, pref, align):
    t = min(pref, n)
    t -= t % align
    while t >= align:
        if n % t == 0:
            return t
        t -= align
    return n


def _nbytes(shape, dtype):
    return math.prod(shape) * jnp.dtype(dtype).itemsize


def _params(sem, blocks, scratch=0):
    need = 2 * sum(blocks) + scratch
    limit = min(VMEM_BUDGET, max(need + need // 4, 16 << 20))
    return pltpu.CompilerParams(dimension_semantics=sem, vmem_limit_bytes=int(limit))


def _sigmoid(x):
    return 1.0 / (1.0 + jnp.exp(-x))


def _softplus(x):
    return jnp.maximum(x, 0.0) + jnp.log(1.0 + jnp.exp(-jnp.abs(x)))


def _dot(a, b, **kw):
    return jnp.dot(a, b, preferred_element_type=F32, **kw)


def _rmsnorm_kernel(x_ref, g_ref, o_ref):
    x = x_ref[...]
    y = x * lax.rsqrt(jnp.mean(x * x, axis=-1, keepdims=True) + NORM_EPS)
    o_ref[...] = (y * g_ref[...]).astype(o_ref.dtype)


def rmsnorm(x, g, out_dtype):
    m, d = x.shape
    tr = _tile(m, 256, SUBLANE)
    return pl.pallas_call(
        _rmsnorm_kernel,
        grid=(m // tr,),
        in_specs=[pl.BlockSpec((tr, d), lambda i: (i, 0)), pl.BlockSpec((1, d), lambda i: (0, 0))],
        out_specs=pl.BlockSpec((tr, d), lambda i: (i, 0)),
        out_shape=jax.ShapeDtypeStruct((m, d), out_dtype),
        compiler_params=_params(("arbitrary",), [_nbytes((tr, d), F32), _nbytes((tr, d), out_dtype)]),
        name="rmsnorm",
    )(x, g.reshape(1, d))


def _mm_kernel(*refs, n_a, has_res, cast_w):
    a_refs = refs[:n_a]
    w_ref = refs[n_a]
    res_ref = refs[n_a + 1] if has_res else None
    o_ref = refs[n_a + 1 + has_res]
    if cast_w:
        wb_ref = refs[n_a + 2 + has_res]

        @pl.when(pl.program_id(1) == 0)
        def _():
            wb_ref[...] = w_ref[...].astype(BF16)
        w_ref = wb_ref
    acc = None
    off = 0
    for a_ref in a_refs:
        ka = a_ref.shape[1]
        part = _dot(a_ref[...], w_ref[off:off + ka, :])
        acc = part if acc is None else acc + part
        off += ka
    if has_res:
        acc = acc + res_ref[...]
    o_ref[...] = acc.astype(o_ref.dtype)


def matmul(a_list, w, n_out, res=None, out_dtype=F32, tm_pref=1024, tn_pref=512, layer=None, name="matmul"):
    m = a_list[0].shape[0]
    k = w.shape[-2]
    assert sum(a.shape[1] for a in a_list) == k and (w.ndim == 2) == (layer is None)
    tm = _tile(m, tm_pref, 16)
    tn = _tile(n_out, tn_pref, LANE)
    cast_w = w.dtype != BF16
    in_specs = [pl.BlockSpec((tm, a.shape[1]), lambda j, i: (i, 0)) for a in a_list]
    if layer is None:
        in_specs.append(pl.BlockSpec((k, tn), lambda j, i: (0, j)))
    else:
        in_specs.append(pl.BlockSpec((None, k, tn), lambda j, i: (layer, 0, j)))
    args = list(a_list) + [w]
    blocks = [_nbytes((tm, k), BF16), _nbytes((k, tn), w.dtype), _nbytes((tm, tn), out_dtype)]
    if res is not None:
        in_specs.append(pl.BlockSpec((tm, tn), lambda j, i: (i, j)))
        args.append(res)
        blocks.append(_nbytes((tm, tn), F32))
    scratch = [pltpu.VMEM((k, tn), BF16)] if cast_w else []
    return pl.pallas_call(
        functools.partial(_mm_kernel, n_a=len(a_list), has_res=res is not None, cast_w=cast_w),
        grid=(n_out // tn, m // tm),
        in_specs=in_specs,
        out_specs=pl.BlockSpec((tm, tn), lambda j, i: (i, j)),
        out_shape=jax.ShapeDtypeStruct((m, n_out), out_dtype),
        scratch_shapes=scratch,
        compiler_params=_params(("arbitrary", "arbitrary"), blocks,
                                _nbytes((k, tn), BF16) * cast_w + 2 * _nbytes((tm, tn), F32)),
        name=name,
    )(*args)


def _gateup_kernel(a_ref, wg_ref, wu_ref, o_ref, wgb_ref, wub_ref):
    @pl.when(pl.program_id(1) == 0)
    def _():
        wgb_ref[...] = wg_ref[...].astype(BF16)
        wub_ref[...] = wu_ref[...].astype(BF16)
    a = a_ref[...]
    g = _dot(a, wgb_ref[...])
    u = _dot(a, wub_ref[...])
    o_ref[...] = (g * _sigmoid(g) * u).astype(o_ref.dtype)


def gateup(a, wg, wu, tm_pref=1024, tn_pref=256):
    m, k = a.shape
    f = wg.shape[1]
    tm = _tile(m, tm_pref, 16)
    tn = _tile(f, tn_pref, LANE)
    blocks = [_nbytes((tm, k), BF16), 2 * _nbytes((k, tn), F32), _nbytes((tm, tn), BF16)]
    return pl.pallas_call(
        _gateup_kernel,
        grid=(f // tn, m // tm),
        in_specs=[pl.BlockSpec((tm, k), lambda j, i: (i, 0)),
                  pl.BlockSpec((k, tn), lambda j, i: (0, j)),
                  pl.BlockSpec((k, tn), lambda j, i: (0, j))],
        out_specs=pl.BlockSpec((tm, tn), lambda j, i: (i, j)),
        out_shape=jax.ShapeDtypeStruct((m, f), BF16),
        scratch_shapes=[pltpu.VMEM((k, tn), BF16), pltpu.VMEM((k, tn), BF16)],
        compiler_params=_params(("arbitrary", "arbitrary"), blocks,
                                2 * _nbytes((k, tn), BF16) + 3 * _nbytes((tm, tn), F32)),
        name="ffn_gateup",
    )(a, wg, wu)


def _block_diag_ones(n, block, dtype):
    r = lax.broadcasted_iota(jnp.int32, (n, n), 0) // block
    c = lax.broadcasted_iota(jnp.int32, (n, n), 1) // block
    return jnp.where(r == c, 1.0, 0.0).astype(dtype)


def _head_sums(xs):
    x = xs[0] if len(xs) == 1 else jnp.concatenate(xs, axis=0)
    w = x.shape[1]
    gl = UNIT_LANES if w % UNIT_LANES == 0 else LANE
    bd = _block_diag_ones(gl, RW_HEAD, BF16)
    hi = x.astype(BF16)
    lo = (x - hi.astype(F32)).astype(BF16)
    parts = [_dot(hi[:, s:s + gl], bd) + _dot(lo[:, s:s + gl], bd) for s in range(0, w, gl)]
    out = parts[0] if len(parts) == 1 else jnp.concatenate(parts, axis=1)
    n = xs[0].shape[0]
    return [out[i * n:(i + 1) * n] for i in range(len(xs))]


def _rw_prep_kernel(*refs, rw, seg, has_vfirst, n_prompt, t_prompt):
    if has_vfirst:
        (p_ref, prev_ref, vf_ref, mu_ref, w2_ref, w0_ref, a2_ref, a0_ref, g2_ref, kk_ref, ka_ref,
         v2_ref, v0_ref, r_out, w_out, k_out, v_out, a_out, b_out, g_out) = refs
    else:
        (p_ref, prev_ref, mu_ref, w2_ref, w0_ref, a2_ref, a0_ref, g2_ref, kk_ref, ka_ref,
         r_out, w_out, k_out, v_out, a_out, b_out, g_out) = refs
    tr = p_ref.shape[0]
    row0 = pl.program_id(0) * tr
    rows = lax.broadcasted_iota(jnp.int32, (tr, 1), 0)
    grow = rows + row0
    seq_start = jnp.logical_and(grow < n_prompt, grow % t_prompt == 0)

    def shifted_lerp(lo, width):
        p = p_ref[:, lo:lo + width]
        above = pltpu.roll(p, 1, axis=0)
        above = jnp.where(rows == 0, prev_ref[SUBLANE - 1:SUBLANE, lo:lo + width], above)
        above = jnp.where(seq_start, 0.0, above)
        return p + mu_ref[:, lo:lo + width] * (above - p)

    o_r, o_k, o_v, o_w, w_w, o_a, w_a, o_g, w_g, o_xv, w_xv = seg
    r = shifted_lerp(o_r, rw)
    kr = shifted_lerp(o_k, rw)
    vr = shifted_lerp(o_v, rw)
    xw = shifted_lerp(o_w, w_w)
    xa = shifted_lerp(o_a, w_a)
    xg = shifted_lerp(o_g, w_g)

    w_log = -_softplus(-(w0_ref[...] + _dot(jnp.tanh(xw).astype(BF16), w2_ref[...]))) - 0.5
    decay = jnp.exp(-jnp.exp(w_log))
    a = _sigmoid(a0_ref[...] + _dot(xa.astype(BF16), a2_ref[...]))
    g = _dot(_sigmoid(xg).astype(BF16), g2_ref[...])
    if has_vfirst:
        xv = shifted_lerp(o_xv, w_xv)
        vr = vr + (vf_ref[...] - vr) * _sigmoid(v0_ref[...] + _dot(xv.astype(BF16), v2_ref[...]))
    kk = kr * kk_ref[...]
    kk = kk * lax.rsqrt(jnp.maximum(_head_sums([kk * kk])[0], 1e-24))
    r_out[...] = r
    w_out[...] = decay
    k_out[...] = kr * (1.0 + (a - 1.0) * ka_ref[...])
    v_out[...] = vr
    a_out[...] = -kk
    b_out[...] = kk * a
    g_out[...] = g


def rw_prep(p_rw, seg, rw, mu, w2, w0, a2, a0, g2, kk, ka, v2, v0, v_first, n_prompt, t_prompt):
    m, n_rw = p_rw.shape
    tr = _tile(m, 128, SUBLANE)
    nsub = tr // SUBLANE
    has_vfirst = v_first is not None
    row = lambda w: pl.BlockSpec((1, w), lambda i: (0, 0))
    full = lambda x: pl.BlockSpec(x.shape, lambda i: (0, 0))
    tile = pl.BlockSpec((tr, rw), lambda i: (i, 0))
    in_specs = [pl.BlockSpec((tr, n_rw), lambda i: (i, 0)),
                pl.BlockSpec((SUBLANE, n_rw), lambda i: (jnp.maximum(i * nsub - 1, 0), 0))]
    args = [p_rw, p_rw]
    if has_vfirst:
        in_specs.append(tile)
        args.append(v_first)
    in_specs += [row(n_rw), full(w2), row(rw), full(a2), row(rw), full(g2), row(rw), row(rw)]
    args += [mu.reshape(1, n_rw), w2, w0.reshape(1, rw), a2, a0.reshape(1, rw), g2,
             kk.reshape(1, rw), ka.reshape(1, rw)]
    if has_vfirst:
        in_specs += [full(v2), row(rw)]
        args += [v2, v0.reshape(1, rw)]
    blocks = [_nbytes((tr + SUBLANE, n_rw), F32), (7 + has_vfirst) * _nbytes((tr, rw), F32),
              sum(_nbytes(x.shape, x.dtype) for x in (w2, a2, g2)) + (_nbytes(v2.shape, v2.dtype) if has_vfirst else 0)]
    outs = pl.pallas_call(
        functools.partial(_rw_prep_kernel, rw=rw, seg=seg, has_vfirst=has_vfirst,
                          n_prompt=n_prompt, t_prompt=t_prompt),
        grid=(m // tr,),
        in_specs=in_specs,
        out_specs=[tile] * 7,
        out_shape=[jax.ShapeDtypeStruct((m, rw), F32)] * 7,
        compiler_params=_params(("arbitrary",), blocks, 12 * _nbytes((tr, rw), F32)),
        name="rwkv_prep",
    )(*args)
    return outs


def _rw_scan_kernel(*refs, nb, n_groups, tc, t_start, split_in, layer, owns_stack):
    n_op = 7 * (nb if split_in else 1)
    op_refs = refs[:n_op]
    gnw_ref, gnb_ref, rk_ref, s0_ref = refs[n_op:n_op + 4]
    o_ref, sout_ref, s_scr, lhs_scr, q_scr, y8_scr, y_scr = refs[-7:]
    c = pl.program_id(1)
    units = [(s, grp) for s in range(nb) for grp in range(n_groups)]

    def operand(kind, s):
        return (op_refs[kind * nb + s], 0) if split_in else (op_refs[kind], s * tc)

    @pl.when(c == 0)
    def _():
        for u, (s, grp) in enumerate(units):
            s_scr[u * RW_HEAD:(u + 1) * RW_HEAD, :] = jnp.concatenate(
                [s0_ref[s, UNIT_HEADS * grp + h] for h in range(UNIT_HEADS)], axis=1)

    bd = _block_diag_ones(UNIT_LANES, RW_HEAD, BF16)
    diag = (lax.broadcasted_iota(jnp.int32, (RW_HEAD, UNIT_LANES), 1) % RW_HEAD
            == lax.broadcasted_iota(jnp.int32, (RW_HEAD, UNIT_LANES), 0))

    def group_step(base, first):
        def row(kind, u, j):
            s, grp = units[u]
            ref, off = operand(kind, s)
            return ref[pl.ds(off + base, SUBLANE), grp * UNIT_LANES:(grp + 1) * UNIT_LANES][j:j + 1, :]

        if first:
            y8_scr[...] = jnp.zeros(y8_scr.shape, F32)
        for j in range(first, SUBLANE):
            par = j % 2
            for u in range(len(units)):
                s = s_scr[u * RW_HEAD:(u + 1) * RW_HEAD, :]
                lhs_scr[par, 2 * u * RW_HEAD:(2 * u + 1) * RW_HEAD, :] = (s * row(4, u, j)).astype(BF16)
                lhs_scr[par, (2 * u + 1) * RW_HEAD:(2 * u + 2) * RW_HEAD, :] = (
                    jnp.where(diag, row(3, u, j), 0.0).astype(BF16))
            bc = _dot(lhs_scr[par], bd)
            for u in range(len(units)):
                rows = slice(u * RW_HEAD, (u + 1) * RW_HEAD)
                s = (s_scr[rows, :] * row(1, u, j)
                     + bc[2 * u * RW_HEAD:(2 * u + 1) * RW_HEAD] * row(5, u, j)
                     + bc[(2 * u + 1) * RW_HEAD:(2 * u + 2) * RW_HEAD] * row(2, u, j))
                s_scr[rows, :] = s
                q_scr[par, rows, :] = (s * row(0, u, j)).astype(BF16)
            y_bc = _dot(q_scr[par], bd)
            for u in range(len(units)):
                y_row = jnp.sum(jnp.where(diag, y_bc[u * RW_HEAD:(u + 1) * RW_HEAD], 0.0), axis=0, keepdims=True)
                y8_scr[u * SUBLANE + j:u * SUBLANE + j + 1, :] = y_row
        for u, (s, grp) in enumerate(units):
            y_scr[pl.ds(s * tc + base, SUBLANE), grp * UNIT_LANES:(grp + 1) * UNIT_LANES] = (
                y8_scr[u * SUBLANE:(u + 1) * SUBLANE, :])

    if tc == SUBLANE:
        group_step(0, t_start)
    else:
        assert t_start == 0 and tc % SUBLANE == 0

        def body(i, carry):
            group_step(pl.multiple_of(i * SUBLANE, SUBLANE), 0)
            return carry

        lax.fori_loop(0, tc // SUBLANE, body, 0)

    for s in range(nb):
        def full(kind):
            ref, off = operand(kind, s)
            return ref[off:off + tc, :]
        y = y_scr[s * tc:(s + 1) * tc, :]
        mean, rk_sum = _head_sums([y, full(0) * full(2) * rk_ref[...]])
        yc = y - mean * (1.0 / RW_HEAD)
        var = _head_sums([yc * yc])[0] * (1.0 / RW_HEAD)
        yn = yc * lax.rsqrt(var + GN_EPS) * gnw_ref[...] + gnb_ref[...]
        o_ref[s] = ((yn + rk_sum * full(3)) * full(6)).astype(o_ref.dtype)

    @pl.when(c == pl.num_programs(1) - 1)
    def _():
        out = _state_out(sout_ref, layer, owns_stack)
        for u, (s, grp) in enumerate(units):
            st = s_scr[u * RW_HEAD:(u + 1) * RW_HEAD, :]
            for h in range(UNIT_HEADS):
                out[s, UNIT_HEADS * grp + h] = st[:, h * RW_HEAD:(h + 1) * RW_HEAD]


def _state_io(s0, s0_layer, layer, depth, s_prev, nb, n_inputs):
    blk = (None, nb) + s0.shape[2:]
    zeros = (0,) * (s0.ndim - 2)
    in_spec = pl.BlockSpec(blk, lambda i, c: (s0_layer, i) + zeros)
    out_shape = jax.ShapeDtypeStruct((depth,) + s0.shape[1:], s0.dtype)
    if s_prev is None:
        out_spec = pl.BlockSpec((depth, nb) + s0.shape[2:], lambda i, c: (0, i) + zeros)
        return in_spec, out_spec, out_shape, [], [], {}
    out_spec = pl.BlockSpec(blk, lambda i, c: (layer, i) + zeros)
    return in_spec, out_spec, out_shape, [pl.BlockSpec(memory_space=pl.ANY)], [s_prev], {n_inputs: 1}


def _state_out(sout_ref, layer, owns_stack):
    if not owns_stack:
        return sout_ref
    for other in range(sout_ref.shape[0]):
        if other != layer:
            sout_ref[other] = jnp.zeros(sout_ref.shape[1:], sout_ref.dtype)
    return sout_ref.at[layer]


def rw_scan(ops, g, gn_w, gn_b, rk, s0, s0_layer, layer, depth, s_prev, row0, n_seq, t_seq, tc, t_start, nb):
    rw = ops[0].shape[1]
    n_heads = s0.shape[2]
    assert n_heads % UNIT_HEADS == 0
    n_groups = n_heads // UNIT_HEADS
    n_chunks = t_seq // tc
    split_in = n_chunks > 1
    assert n_seq % nb == 0 and (split_in or (row0 % (nb * tc) == 0 and tc == t_seq))
    if split_in:
        base = row0 // tc
        in_specs = [pl.BlockSpec((tc, rw), lambda i, c, s=s: (base + (i * nb + s) * n_chunks + c, 0))
                    for _ in range(7) for s in range(nb)]
        args = [x for x in (*ops, g) for _ in range(nb)]
    else:
        base = row0 // (nb * tc)
        in_specs = [pl.BlockSpec((nb * tc, rw), lambda i, c: (base + i, 0))] * 7
        args = [*ops, g]
    row = pl.BlockSpec((1, rw), lambda i, c: (0, 0))
    st_in, st_out, st_shape, prev_spec, prev_arg, alias = _state_io(
        s0, s0_layer, layer, depth, s_prev, nb, len(args) + 4)
    n_units = nb * n_groups
    scratch = [pltpu.VMEM((n_units * RW_HEAD, UNIT_LANES), F32),
               pltpu.VMEM((2, 2 * n_units * RW_HEAD, UNIT_LANES), BF16),
               pltpu.VMEM((2, n_units * RW_HEAD, UNIT_LANES), BF16),
               pltpu.VMEM((n_units * SUBLANE, UNIT_LANES), F32),
               pltpu.VMEM((nb * tc, rw), F32)]
    blocks = [7 * _nbytes((nb * tc, rw), F32), _nbytes((nb * tc, rw), BF16),
              (1 + depth) * _nbytes((nb * n_heads, RW_HEAD, LANE), F32)]
    o, s_out = pl.pallas_call(
        functools.partial(_rw_scan_kernel, nb=nb, n_groups=n_groups, tc=tc, t_start=t_start, split_in=split_in,
                          layer=layer, owns_stack=s_prev is None),
        grid=(n_seq // nb, n_chunks),
        in_specs=in_specs + [row] * 3 + [st_in] + prev_spec,
        out_specs=[pl.BlockSpec((nb, tc, rw), lambda i, c: (i, c, 0)), st_out],
        out_shape=[jax.ShapeDtypeStruct((n_seq, t_seq, rw), BF16), st_shape],
        input_output_aliases=alias,
        scratch_shapes=scratch,
        compiler_params=_params(("arbitrary", "arbitrary"), blocks,
                                16 * _nbytes((n_units * RW_HEAD, UNIT_LANES), F32) + 8 * _nbytes((tc, rw), F32)),
        name="rwkv_scan",
    )(*args, gn_w.reshape(1, rw), gn_b.reshape(1, rw), rk.reshape(1, rw), s0, *prev_arg)
    return o.reshape(n_seq * t_seq, rw), s_out


def _gla_kernel(*refs, n_heads, dk, dv, t_start, layer, owns_stack):
    q_ref, k_ref, v_ref, go_ref, lr_ref, w2_ref, gb_ref, gn_ref, s0_ref = refs[:9]
    o_ref, sout_ref, s_scr = refs[-3:]
    c_rows = q_ref.shape[0]
    n = pl.program_id(1)

    @pl.when(n == 0)
    def _():
        s_scr[...] = s0_ref[0]

    rows = lax.broadcasted_iota(jnp.int32, (c_rows, 1), 0)
    valid = rows >= t_start
    tri = (lax.broadcasted_iota(jnp.int32, (c_rows, c_rows), 0)
           >= lax.broadcasted_iota(jnp.int32, (c_rows, c_rows), 1))
    tri_f = jnp.where(tri, 1.0, 0.0).astype(F32)
    lr = lr_ref[...]
    mid = max(c_rows // 2 - 1, 0)
    scale = dk ** -0.5
    for h in range(n_heads):
        ks = slice(h * dk, (h + 1) * dk)
        vs = slice(h * dv, (h + 1) * dv)
        x = _dot(lr, w2_ref[:, ks], precision=HIGHEST) + gb_ref[:, ks]
        log_a = (jnp.minimum(x, 0.0) - jnp.log(1.0 + jnp.exp(-jnp.abs(x)))) * (1.0 / GLA_GATE_TAU)
        log_a = jnp.where(valid, log_a, 0.0)
        b = _dot(tri_f, log_a, precision=HIGHEST)
        b_last = b[c_rows - 1:c_rows, :]
        b_mid = b[mid:mid + 1, :]
        q = q_ref[:, ks] * scale
        k = jnp.where(valid, k_ref[:, ks], 0.0)
        v = v_ref[:, vs].astype(BF16)
        qi = (q * jnp.exp(b - b_mid)).astype(BF16)
        ki = (k * jnp.exp(b_mid - b)).astype(BF16)
        att = lax.dot_general(qi, ki, (((1,), (1,)), ((), ())), preferred_element_type=F32)
        att = jnp.where(tri, att, 0.0).astype(BF16)
        s = s_scr[h]
        o = _dot(att, v) + _dot((q * jnp.exp(b)).astype(BF16), s.astype(BF16))
        kd = (k * jnp.exp(b_last - b)).astype(BF16)
        upd = lax.dot_general(kd, v, (((0,), (0,)), ((), ())), preferred_element_type=F32)
        dcol = jnp.transpose(jnp.broadcast_to(jnp.exp(b_last), (LANE, dk)))
        s_scr[h] = s * jnp.concatenate([dcol] * (dv // LANE), axis=1) + upd
        on = o * lax.rsqrt(jnp.mean(o * o, axis=-1, keepdims=True) + HEAD_NORM_EPS)
        go = go_ref[:, vs]
        o_ref[:, vs] = (on * gn_ref[:, vs] * (go * _sigmoid(go))).astype(o_ref.dtype)

    @pl.when(n == pl.num_programs(1) - 1)
    def _():
        _state_out(sout_ref, layer, owns_stack)[0] = s_scr[...]


def gla(p_gla, p_rw, lr_block, w2p, gb, gn, s0, s0_layer, layer, depth, s_prev, row0, n_seq, t_seq, c_rows, t_start):
    _, _, n_heads, dk, dv = s0.shape
    qk, gw = n_heads * dk, n_heads * dv
    n_chunks = t_seq // c_rows
    base = row0 // c_rows
    rmap = lambda col: (lambda i, n: (base + i * n_chunks + n, col))
    st_in, st_out, st_shape, prev_spec, prev_arg, alias = _state_io(s0, s0_layer, layer, depth, s_prev, 1, 9)
    const = lambda shape: pl.BlockSpec(shape, lambda i, n: (0, 0))
    blocks = [_nbytes((c_rows, 2 * qk + 2 * gw + LANE), F32), _nbytes((LANE + 2, qk), F32),
              _nbytes((c_rows, gw), BF16), (1 + depth) * _nbytes((n_heads, dk, dv), F32)]
    o, s_out = pl.pallas_call(
        functools.partial(_gla_kernel, n_heads=n_heads, dk=dk, dv=dv, t_start=t_start,
                          layer=layer, owns_stack=s_prev is None),
        grid=(n_seq, n_chunks),
        in_specs=[pl.BlockSpec((c_rows, qk), rmap(0)),
                  pl.BlockSpec((c_rows, qk), rmap(1)),
                  pl.BlockSpec((c_rows, gw), rmap(2 * qk // gw)),
                  pl.BlockSpec((c_rows, gw), rmap(2 * qk // gw + 1)),
                  pl.BlockSpec((c_rows, LANE), rmap(lr_block)),
                  const((LANE, qk)), const((1, qk)), const((1, gw)), st_in] + prev_spec,
        out_specs=[pl.BlockSpec((c_rows, gw), lambda i, n: (i * n_chunks + n, 0)), st_out],
        out_shape=[jax.ShapeDtypeStruct((n_seq * t_seq, gw), BF16), st_shape],
        input_output_aliases=alias,
        scratch_shapes=[pltpu.VMEM((n_heads, dk, dv), F32)],
        compiler_params=_params(("arbitrary", "arbitrary"), blocks,
                                _nbytes((n_heads, dk, dv), F32) + 6 * _nbytes((dk, dv), F32)),
        name="gla",
    )(p_gla, p_gla, p_gla, p_gla, p_rw, w2p, gb.reshape(1, qk), gn.reshape(1, gw), s0, *prev_arg)
    return o, s_out


def _router_kernel(x_ref, g_ref, wr_ref, h_ref, info_ref, *, n_experts):
    x = x_ref[...]
    h = x * lax.rsqrt(jnp.mean(x * x, axis=-1, keepdims=True) + NORM_EPS) * g_ref[...]
    half = h.shape[1] // 2
    lo = lax.bitcast_convert_type(h[:, :half].astype(BF16).astype(F32), jnp.uint32)
    hi = lax.bitcast_convert_type(h[:, half:].astype(BF16).astype(F32), jnp.uint32)
    h_ref[...] = (lo >> 16) | (hi & jnp.uint32(0xFFFF0000))
    logits = _dot(h, wr_ref[...], precision=HIGHEST)
    lane = lax.broadcasted_iota(jnp.int32, logits.shape, 1)
    neg = jnp.float32(-jnp.inf)
    logits = jnp.where(lane < n_experts, logits, neg)
    m1 = jnp.max(logits, axis=-1, keepdims=True)
    i1 = jnp.min(jnp.where(logits == m1, lane, LANE), axis=-1, keepdims=True)
    rest = jnp.where(lane == i1, neg, logits)
    m2 = jnp.max(rest, axis=-1, keepdims=True)
    i2 = jnp.min(jnp.where(rest == m2, lane, LANE), axis=-1, keepdims=True)
    e2 = jnp.exp(m2 - m1)
    g1 = 1.0 / (1.0 + e2)
    g2 = e2 / (1.0 + e2)
    info = jnp.where(lane == 0, i1.astype(F32),
                     jnp.where(lane == 1, i2.astype(F32),
                               jnp.where(lane == 2, g1, jnp.where(lane == 3, g2, 0.0))))
    info_ref[...] = info


def moe_router(x, g, router):
    m, d = x.shape
    n_experts = router.shape[1]
    tr = _tile(m, 256, SUBLANE)
    wr = jnp.zeros((d, LANE), F32).at[:, :n_experts].set(router)
    blocks = [_nbytes((tr, d), F32), _nbytes((d, LANE), F32), _nbytes((tr, d // 2), jnp.uint32)]
    return pl.pallas_call(
        functools.partial(_router_kernel, n_experts=n_experts),
        grid=(m // tr,),
        in_specs=[pl.BlockSpec((tr, d), lambda i: (i, 0)), pl.BlockSpec((1, d), lambda i: (0, 0)),
                  pl.BlockSpec((d, LANE), lambda i: (0, 0))],
        out_specs=[pl.BlockSpec((tr, d // 2), lambda i: (i, 0)), pl.BlockSpec((tr, LANE), lambda i: (i, 0))],
        out_shape=[jax.ShapeDtypeStruct((m, d // 2), jnp.uint32), jax.ShapeDtypeStruct((m, LANE), F32)],
        compiler_params=_params(("arbitrary",), blocks, 4 * _nbytes((tr, d), F32)),
        name="moe_router",
    )(x, g.reshape(1, d), wr)


def _gather_kernel(idx_ref, src_ref, out_ref, buf, sem):
    tg, w = buf.shape

    def copy(r):
        return pltpu.make_async_copy(src_ref.at[pl.ds(idx_ref[0, 0, r], 1)], buf.at[pl.ds(r, 1)], sem)

    def start(r, carry):
        copy(r).start()
        return carry

    def wait(r, carry):
        copy(r).wait()
        return carry

    lax.fori_loop(0, tg, start, 0)
    lax.fori_loop(0, tg, wait, 0)
    words = buf[...]
    out_ref[:, :w] = lax.bitcast_convert_type(words << 16, F32).astype(BF16)
    out_ref[:, w:] = lax.bitcast_convert_type(words & jnp.uint32(0xFFFF0000), F32).astype(BF16)


def gather_rows(src, idx, tg=256):
    n = idx.shape[0]
    tg = _tile(n, tg, 16)
    w = src.shape[1]
    return pl.pallas_call(
        _gather_kernel,
        grid=(n // tg,),
        in_specs=[pl.BlockSpec((1, 1, tg), lambda i: (i, 0, 0), memory_space=pltpu.SMEM),
                  pl.BlockSpec(memory_space=pl.ANY)],
        out_specs=pl.BlockSpec((tg, 2 * w), lambda i: (i, 0)),
        out_shape=jax.ShapeDtypeStruct((n, 2 * w), BF16),
        scratch_shapes=[pltpu.VMEM((tg, w), src.dtype), pltpu.SemaphoreType.DMA(())],
        compiler_params=_params(("arbitrary",), [_nbytes((tg, 2 * w), BF16)], 4 * _nbytes((tg, w), src.dtype)),
        name="moe_gather",
    )(idx.reshape(n // tg, 1, tg), src)


def _moe_gateup_kernel(te_ref, tf_ref, tv_ref, a_ref, wg_ref, wu_ref, o_ref, wgb_ref, wub_ref):
    t = pl.program_id(1)

    @pl.when(tf_ref[t] == 1)
    def _():
        wgb_ref[...] = wg_ref[...].astype(BF16)
        wub_ref[...] = wu_ref[...].astype(BF16)

    @pl.when(tv_ref[t] == 1)
    def _():
        a = a_ref[...]
        g = _dot(a, wgb_ref[...])
        u = _dot(a, wub_ref[...])
        o_ref[...] = (g * _sigmoid(g) * u).astype(o_ref.dtype)

    @pl.when(tv_ref[t] == 0)
    def _():
        o_ref[...] = jnp.zeros(o_ref.shape, o_ref.dtype)


def moe_gateup(xs, wg, wu, tile_expert, tile_first, tile_valid, tm, tn_pref=512):
    r, d = xs.shape
    f = wg.shape[2]
    tn = _tile(f, tn_pref, LANE)
    wspec = pl.BlockSpec((None, d, tn), lambda j, t, te, tf, tv: (te[t], 0, j))
    blocks = [_nbytes((tm, d), BF16), 2 * _nbytes((d, tn), F32), _nbytes((tm, tn), BF16)]
    return pl.pallas_call(
        _moe_gateup_kernel,
        grid_spec=pltpu.PrefetchScalarGridSpec(
            num_scalar_prefetch=3,
            grid=(f // tn, r // tm),
            in_specs=[pl.BlockSpec((tm, d), lambda j, t, te, tf, tv: (t, 0)), wspec, wspec],
            out_specs=pl.BlockSpec((tm, tn), lambda j, t, te, tf, tv: (t, j)),
            scratch_shapes=[pltpu.VMEM((d, tn), BF16), pltpu.VMEM((d, tn), BF16)]),
        out_shape=jax.ShapeDtypeStruct((r, f), BF16),
        compiler_params=_params(("arbitrary", "arbitrary"), blocks,
                                2 * _nbytes((d, tn), BF16) + 3 * _nbytes((tm, tn), F32)),
        name="moe_gateup",
    )(tile_expert, tile_first, tile_valid, xs, wg, wu)


def _moe_down_kernel(te_ref, tf_ref, tv_ref, a_ref, w_ref, o_ref, wb_ref):
    t = pl.program_id(1)
    k = pl.program_id(2)

    @pl.when(tf_ref[t] == 1)
    def _():
        wb_ref[k] = w_ref[...].astype(BF16)

    @pl.when(k == 0)
    def _():
        o_ref[...] = jnp.zeros(o_ref.shape, o_ref.dtype)

    @pl.when(tv_ref[t] == 1)
    def _():
        o_ref[...] += _dot(a_ref[...], wb_ref[k])


def moe_down(act, wd, tile_expert, tile_first, tile_valid, tm, tn_pref=512, tk_pref=3584):
    r, f = act.shape
    d = wd.shape[2]
    tn = _tile(d, tn_pref, LANE)
    tk = _tile(f, tk_pref, LANE)
    nk = f // tk

    def wmap(j, t, k, te, tf, tv):
        return (te[t], jnp.where(tf[t] == 1, k, nk - 1), j)

    blocks = [_nbytes((tm, tk), BF16), _nbytes((tk, tn), F32), _nbytes((tm, tn), F32)]
    return pl.pallas_call(
        _moe_down_kernel,
        grid_spec=pltpu.PrefetchScalarGridSpec(
            num_scalar_prefetch=3,
            grid=(d // tn, r // tm, nk),
            in_specs=[pl.BlockSpec((tm, tk), lambda j, t, k, te, tf, tv: (t, k)),
                      pl.BlockSpec((None, tk, tn), wmap)],
            out_specs=pl.BlockSpec((tm, tn), lambda j, t, k, te, tf, tv: (t, j)),
            scratch_shapes=[pltpu.VMEM((nk, tk, tn), BF16)]),
        out_shape=jax.ShapeDtypeStruct((r, d), F32),
        compiler_params=_params(("arbitrary", "arbitrary", "arbitrary"), blocks,
                                _nbytes((f, tn), BF16) + _nbytes((tm, tn), F32)),
        name="moe_down",
    )(tile_expert, tile_first, tile_valid, act, wd)


def _combine_kernel(p1_ref, p2_ref, x_ref, info_ref, ys_ref, o_ref, buf1, buf2, sem):
    tc = x_ref.shape[0]

    def copies(r):
        return (pltpu.make_async_copy(ys_ref.at[pl.ds(p1_ref[0, 0, r], 1)], buf1.at[pl.ds(r, 1)], sem.at[0]),
                pltpu.make_async_copy(ys_ref.at[pl.ds(p2_ref[0, 0, r], 1)], buf2.at[pl.ds(r, 1)], sem.at[1]))

    def start(r, carry):
        c1, c2 = copies(r)
        c1.start()
        c2.start()
        return carry

    def wait(r, carry):
        c1, c2 = copies(r)
        c1.wait()
        c2.wait()
        return carry

    lax.fori_loop(0, tc, start, 0)
    lax.fori_loop(0, tc, wait, 0)
    info = info_ref[...]
    o_ref[...] = x_ref[...] + info[:, 2:3] * buf1[...] + info[:, 3:4] * buf2[...]


def moe_combine(x, info, ys, pos1, pos2, tc=128):
    m, d = x.shape
    tc = _tile(m, tc, SUBLANE)
    idx = pl.BlockSpec((1, 1, tc), lambda i: (i, 0, 0), memory_space=pltpu.SMEM)
    blocks = [2 * _nbytes((tc, d), F32), _nbytes((tc, LANE), F32)]
    return pl.pallas_call(
        _combine_kernel,
        grid=(m // tc,),
        in_specs=[idx, idx, pl.BlockSpec((tc, d), lambda i: (i, 0)), pl.BlockSpec((tc, LANE), lambda i: (i, 0)),
                  pl.BlockSpec(memory_space=pl.ANY)],
        out_specs=pl.BlockSpec((tc, d), lambda i: (i, 0)),
        out_shape=jax.ShapeDtypeStruct((m, d), F32),
        scratch_shapes=[pltpu.VMEM((tc, d), F32), pltpu.VMEM((tc, d), F32), pltpu.SemaphoreType.DMA((2,))],
        compiler_params=_params(("arbitrary",), blocks, 4 * _nbytes((tc, d), F32)),
        name="moe_combine",
    )(pos1.reshape(m // tc, 1, tc), pos2.reshape(m // tc, 1, tc), x, info, ys)


def moe_ffn(x, norm_g, router, wg, wu, wd, row_valid, n_valid, tm=512):
    m, d = x.shape
    n_experts = router.shape[1]
    h_packed, info = moe_router(x, norm_g, router)

    none = n_experts
    e1 = jnp.where(row_valid, info[:, 0].astype(jnp.int32), none)
    e2 = jnp.where(row_valid, info[:, 1].astype(jnp.int32), none)
    e_all = jnp.concatenate([e1, e2])
    onehot = (e_all[:, None] == jnp.arange(n_experts, dtype=jnp.int32)[None, :]).astype(jnp.int32)
    counts = jnp.sum(onehot, axis=0)
    rank = jnp.sum((jnp.cumsum(onehot, axis=0) - onehot) * onehot, axis=1)
    padded = (counts + tm - 1) // tm * tm
    ends = jnp.cumsum(padded)
    starts = ends - padded
    n_rows = _round_up(2 * n_valid + n_experts * (tm - 1), tm)
    n_tiles = n_rows // tm
    pos = jnp.where(e_all < none, jnp.take(starts, jnp.minimum(e_all, none - 1)) + rank, n_rows)
    tok = jnp.tile(jnp.arange(m, dtype=jnp.int32), 2)
    row_src = jnp.zeros((n_rows,), jnp.int32).at[pos].set(tok, mode="drop")
    tile_start = jnp.arange(n_tiles, dtype=jnp.int32) * tm
    tile_valid = (tile_start < ends[-1]).astype(jnp.int32)
    last_tile = jnp.maximum(ends[-1] // tm - 1, 0)
    tile_expert = jnp.searchsorted(ends, jnp.minimum(tile_start, last_tile * tm), side="right").astype(jnp.int32)
    tile_expert = jnp.minimum(tile_expert, n_experts - 1)
    prev_expert = jnp.concatenate([jnp.full((1,), -1, jnp.int32), tile_expert[:-1]])
    tile_first = jnp.logical_and(tile_expert != prev_expert, tile_valid == 1).astype(jnp.int32)
    tile_first = tile_first.at[0].set(1)
    pos_safe = jnp.where(pos < n_rows, pos, 0)
    gates = jnp.where(row_valid[:, None], info, 0.0)

    xs = gather_rows(h_packed, row_src)
    act = moe_gateup(xs, wg, wu, tile_expert, tile_first, tile_valid, tm)
    ys = moe_down(act, wd, tile_expert, tile_first, tile_valid, tm)
    return moe_combine(x, gates, ys, pos_safe[:m], pos_safe[m:])


def _pad_cols(w, width):
    return w if w.shape[-1] == width else jnp.pad(w, [(0, 0)] * (w.ndim - 1) + [(0, width - w.shape[-1])])


def _pad_rows(w, rows):
    return w if w.shape[0] == rows else jnp.pad(w, [(0, rows - w.shape[0])] + [(0, 0)] * (w.ndim - 1))


def kernel(x_prompt, x_sample, state_gla, state_rwkv, state_shift, norm_mix, norm_ffn, norm_final,
           w_in_first, w_in_rest, mu_first, mu_rest, gla_w2, gla_b, gla_norm,
           rw_w2, rw_w0, rw_a2, rw_a0, rw_g2, rw_kk, rw_ka, rw_rk, rw_gn_w, rw_gn_b, rw_v2, rw_v0,
           w_out, ffn_w_gate, ffn_w_up, ffn_w_down, moe_router_w, moe_w_gate, moe_w_up, moe_w_down):
    batch, seq, d = x_prompt.shape
    dec_batch, dec_seq, _ = x_sample.shape
    depth = norm_mix.shape[0]
    _, _, gla_heads, dk, dv = state_gla.shape
    rw_heads = state_rwkv.shape[2]
    assert state_rwkv.shape[3] == RW_HEAD and state_rwkv.shape[4] == RW_HEAD and rw_heads % 2 == 0
    qk, gw, rw = gla_heads * dk, gla_heads * dv, rw_heads * RW_HEAD
    rank = gla_w2.shape[1]
    p_gla = 2 * qk + 2 * gw + rank
    assert dk % LANE == 0 and dv % LANE == 0 and (2 * qk) % gw == 0 and rank <= LANE

    group = _round_up(dec_seq + 1, SUBLANE)
    t_start = group - dec_seq
    n_prompt = batch * seq
    m = n_prompt + dec_batch * group
    prompt_chunk = _tile(seq, 64, SUBLANE)
    scan_chunk = _tile(seq, 32, SUBLANE)
    sample_rows = jnp.arange(dec_batch * group, dtype=jnp.int32) % group >= t_start
    row_valid = jnp.concatenate([jnp.ones((n_prompt,), bool), sample_rows])

    x = jnp.concatenate([
        x_prompt.reshape(n_prompt, d),
        jnp.concatenate([jnp.zeros((dec_batch, t_start, d), F32), x_sample], axis=1).reshape(dec_batch * group, d)])

    lw, la, lg = rw_w2.shape[1], rw_a2.shape[1], rw_g2.shape[1]
    lv = rw_v2.shape[1]
    pw, pa, pg, pv = (_round_up(n, LANE) for n in (lw, la, lg, lv))
    o_w = 3 * rw
    o_a = o_w + pw
    o_g = o_a + pa
    o_xv = o_g + pg
    o_lr = o_xv + pv
    n_rw = _round_up(o_lr + LANE, 512)
    seg = (0, rw, 2 * rw, o_w, pw, o_a, pa, o_g, pg, o_xv, pv)

    def repack(w_in_l, mu_l, has_v):
        wr = w_in_l[:, p_gla:]
        mu_l = mu_l
        parts_w, parts_mu = [], []
        src = 0
        for width, padded in ((3 * rw, 3 * rw), (lw, pw), (la, pa), (lg, pg)) + (((lv, pv),) if has_v else ()):
            parts_w.append(_pad_cols(wr[:, src:src + width], padded))
            parts_mu.append(_pad_cols(mu_l[src:src + width], padded))
            src += width
        if not has_v:
            parts_w.append(jnp.zeros((d, pv), F32))
            parts_mu.append(jnp.zeros((pv,), F32))
        parts_w.append(_pad_cols(w_in_l[:, p_gla - rank:p_gla], n_rw - o_lr))
        parts_mu.append(jnp.zeros((n_rw - o_lr,), F32))
        return jnp.concatenate(parts_w, axis=1).astype(BF16), jnp.concatenate(parts_mu)

    zeros_gla = jnp.zeros((1, batch, gla_heads, dk, dv), F32)
    zeros_rw = jnp.zeros((1, batch, rw_heads, RW_HEAD, RW_HEAD), F32)
    last_rows = jnp.concatenate([jnp.arange(batch, dtype=jnp.int32) * seq + seq - 1,
                                 n_prompt + jnp.arange(dec_batch, dtype=jnp.int32) * group + group - 1])
    n_last = _round_up(batch + dec_batch, SUBLANE)
    prev_rows = n_prompt + jnp.arange(dec_batch, dtype=jnp.int32) * group + t_start - 1

    new_gla_p = new_gla_s = new_rw_p = new_rw_s = None
    new_shift = []
    v_first = None
    for l in range(depth):
        h = rmsnorm(x, norm_mix[l], BF16).at[prev_rows].set(state_shift[l].astype(BF16))
        x_last = _pad_rows(jnp.take(x, last_rows, axis=0), n_last)
        new_shift.append(rmsnorm(x_last, norm_mix[l], F32)[:batch + dec_batch])

        if l == 0:
            w_in_l, mu_l = w_in_first, mu_first
        else:
            w_in_l, mu_l = w_in_rest[l - 1], mu_rest[l - 1]
        w_rw, mu_p = repack(w_in_l, mu_l, l > 0)
        pg_all = matmul([h], w_in_l, 2 * qk + 2 * gw, name="w_in_gla")
        pr_all = matmul([h], w_rw, n_rw, name="w_in_rwkv")

        ops = rw_prep(pr_all, seg, rw, mu_p,
                      _pad_rows(rw_w2[l], pw).astype(BF16), rw_w0[l], _pad_rows(rw_a2[l], pa).astype(BF16), rw_a0[l],
                      _pad_rows(rw_g2[l], pg).astype(BF16), rw_kk[l], rw_ka[l],
                      _pad_rows(rw_v2[l - 1], pv).astype(BF16) if l > 0 else None,
                      rw_v0[l - 1] if l > 0 else None, v_first, n_prompt, seq)
        r_, w_, k_, v_, a_, b_, g_ = ops
        if l == 0:
            v_first = v_
        scan_args = ((r_, w_, k_, v_, a_, b_), g_, rw_gn_w[l], rw_gn_b[l], rw_rk[l].reshape(rw))
        o_rw_p, new_rw_p = rw_scan(*scan_args, zeros_rw, 0, l, depth, new_rw_p,
                                   0, batch, seq, scan_chunk, 0, _tile(batch, 2, 1))
        o_rw_s, new_rw_s = rw_scan(*scan_args, state_rwkv, l, l, depth, new_rw_s,
                                   n_prompt, dec_batch, group, group, t_start, _tile(dec_batch, 4, 1))

        w2p = _pad_rows(gla_w2[l], LANE)
        gla_args = (pg_all, pr_all, o_lr // LANE, w2p, gla_b[l], gla_norm[l])
        o_gla_p, new_gla_p = gla(*gla_args, zeros_gla, 0, l, depth, new_gla_p, 0, batch, seq, prompt_chunk, 0)
        o_gla_s, new_gla_s = gla(*gla_args, state_gla, l, l, depth, new_gla_s,
                                 n_prompt, dec_batch, group, group, t_start)

        o_gla = jnp.concatenate([o_gla_p, o_gla_s])
        o_rw = jnp.concatenate([o_rw_p, o_rw_s])
        x = matmul([o_gla, o_rw], w_out, d, res=x, layer=l, name="w_out")

        i = l // 2
        if l % 2 == 0:
            h2 = rmsnorm(x, norm_ffn[l], BF16)
            act = gateup(h2, ffn_w_gate[i], ffn_w_up[i])
            x = matmul([act], ffn_w_down[i].astype(BF16), d, res=x, tm_pref=512, name="ffn_down")
        else:
            x = moe_ffn(x, norm_ffn[l], moe_router_w[i], moe_w_gate[i], moe_w_up[i], moe_w_down[i],
                        row_valid, n_prompt + dec_batch * dec_seq)

    y = rmsnorm(x, norm_final, F32)
    y_prompt = y[:n_prompt].reshape(batch, seq, d)
    y_sample = y[n_prompt:].reshape(dec_batch, group, d)[:, t_start:]
    shift = jnp.stack(new_shift)
    return (y_prompt, y_sample, new_gla_p, new_gla_s, new_rw_p, new_rw_s, shift[:, :batch], shift[:, batch:])
```

```python
import functools
import math

import jax
import jax.numpy as jnp
from jax import lax
from jax.experimental import pallas as pl
from jax.experimental.pallas import tpu as pltpu

F32 = jnp.float32
BF16 = jnp.bfloat16
HIGHEST = lax.Precision.HIGHEST

NORM_EPS = 1e-6
HEAD_NORM_EPS = 1e-5
GN_EPS = 64e-5
GLA_GATE_TAU = 16.0

LANE = 128
SUBLANE = 8
RW_HEAD = 64
MXU_DIM = 256
UNIT_HEADS = MXU_DIM // RW_HEAD
UNIT_LANES = MXU_DIM
VMEM_CAP = 64 << 20
VMEM_BUDGET = VMEM_CAP - (6 << 20)


def _round_up(n, m):
    return (n + m - 1) // m * m


def _tile(n, pref, align):
    t = min(pref, n)
    t -= t % align
    while t >= align:
        if n % t == 0:
            return t
        t -= align
    return n


def _nbytes(shape, dtype):
    return math.prod(shape) * jnp.dtype(dtype).itemsize


def _params(sem, blocks, scratch=0):
    need = 2 * sum(blocks) + scratch
    limit = min(VMEM_BUDGET, max(need + need // 4, 16 << 20))
    return pltpu.CompilerParams(dimension_semantics=sem, vmem_limit_bytes=int(limit))


def _sigmoid(x):
    return 1.0 / (1.0 + jnp.exp(-x))


def _softplus(x):
    return jnp.maximum(x, 0.0) + jnp.log(1.0 + jnp.exp(-jnp.abs(x)))


def _dot(a, b, **kw):
    return jnp.dot(a, b, preferred_element_type=F32, **kw)


def _rmsnorm_kernel(x_ref, g_ref, o_ref):
    x = x_ref[...]
    y = x * lax.rsqrt(jnp.mean(x * x, axis=-1, keepdims=True) + NORM_EPS)
    o_ref[...] = (y * g_ref[...]).astype(o_ref.dtype)


def rmsnorm(x, g, out_dtype, row0=0, rows=None):
    d = x.shape[1]
    m = x.shape[0] if rows is None else rows
    tr = _tile(math.gcd(m, row0) if row0 else m, 256, SUBLANE)
    first = row0 // tr
    return pl.pallas_call(
        _rmsnorm_kernel,
        grid=(m // tr,),
        in_specs=[pl.BlockSpec((tr, d), lambda i: (first + i, 0)), pl.BlockSpec((1, d), lambda i: (0, 0))],
        out_specs=pl.BlockSpec((tr, d), lambda i: (i, 0)),
        out_shape=jax.ShapeDtypeStruct((m, d), out_dtype),
        compiler_params=_params(("arbitrary",), [_nbytes((tr, d), F32), _nbytes((tr, d), out_dtype)]),
        name="rmsnorm",
    )(x, g.reshape(1, d))


def _mm_kernel(*refs, n_a, has_res, cast_w):
    a_refs = refs[:n_a]
    w_ref = refs[n_a]
    res_ref = refs[n_a + 1] if has_res else None
    o_ref = refs[n_a + 1 + has_res]
    if cast_w:
        wb_ref = refs[n_a + 2 + has_res]

        @pl.when(pl.program_id(1) == 0)
        def _():
            wb_ref[...] = w_ref[...].astype(BF16)
        w_ref = wb_ref
    acc = None
    off = 0
    for a_ref in a_refs:
        ka = a_ref.shape[1]
        part = _dot(a_ref[...], w_ref[off:off + ka, :])
        acc = part if acc is None else acc + part
        off += ka
    if has_res:
        acc = acc + res_ref[...]
    o_ref[...] = acc.astype(o_ref.dtype)


def matmul(a_list, w, n_out, res=None, out_dtype=F32, tm_pref=1024, tn_pref=512, layer=None, name="matmul"):
    m = a_list[0].shape[0]
    k = w.shape[-2]
    assert sum(a.shape[1] for a in a_list) == k and (w.ndim == 2) == (layer is None)
    tm = _tile(m, tm_pref, 16)
    tn = _tile(n_out, tn_pref, LANE)
    cast_w = w.dtype != BF16
    in_specs = [pl.BlockSpec((tm, a.shape[1]), lambda j, i: (i, 0)) for a in a_list]
    if layer is None:
        in_specs.append(pl.BlockSpec((k, tn), lambda j, i: (0, j)))
    else:
        in_specs.append(pl.BlockSpec((None, k, tn), lambda j, i: (layer, 0, j)))
    args = list(a_list) + [w]
    blocks = [_nbytes((tm, k), BF16), _nbytes((k, tn), w.dtype), _nbytes((tm, tn), out_dtype)]
    if res is not None:
        in_specs.append(pl.BlockSpec((tm, tn), lambda j, i: (i, j)))
        args.append(res)
        blocks.append(_nbytes((tm, tn), F32))
    scratch = [pltpu.VMEM((k, tn), BF16)] if cast_w else []
    return pl.pallas_call(
        functools.partial(_mm_kernel, n_a=len(a_list), has_res=res is not None, cast_w=cast_w),
        grid=(n_out // tn, m // tm),
        in_specs=in_specs,
        out_specs=pl.BlockSpec((tm, tn), lambda j, i: (i, j)),
        out_shape=jax.ShapeDtypeStruct((m, n_out), out_dtype),
        scratch_shapes=scratch,
        compiler_params=_params(("arbitrary", "arbitrary"), blocks,
                                _nbytes((k, tn), BF16) * cast_w + 2 * _nbytes((tm, tn), F32)),
        name=name,
    )(*args)


def _gateup_kernel(a_ref, wg_ref, wu_ref, o_ref, wgb_ref, wub_ref):
    @pl.when(pl.program_id(1) == 0)
    def _():
        wgb_ref[...] = wg_ref[...].astype(BF16)
        wub_ref[...] = wu_ref[...].astype(BF16)
    a = a_ref[...]
    g = _dot(a, wgb_ref[...])
    u = _dot(a, wub_ref[...])
    o_ref[...] = (g * _sigmoid(g) * u).astype(o_ref.dtype)


def gateup(a, wg, wu, tm_pref=1024, tn_pref=256):
    m, k = a.shape
    f = wg.shape[1]
    tm = _tile(m, tm_pref, 16)
    tn = _tile(f, tn_pref, LANE)
    blocks = [_nbytes((tm, k), BF16), 2 * _nbytes((k, tn), F32), _nbytes((tm, tn), BF16)]
    return pl.pallas_call(
        _gateup_kernel,
        grid=(f // tn, m // tm),
        in_specs=[pl.BlockSpec((tm, k), lambda j, i: (i, 0)),
                  pl.BlockSpec((k, tn), lambda j, i: (0, j)),
                  pl.BlockSpec((k, tn), lambda j, i: (0, j))],
        out_specs=pl.BlockSpec((tm, tn), lambda j, i: (i, j)),
        out_shape=jax.ShapeDtypeStruct((m, f), BF16),
        scratch_shapes=[pltpu.VMEM((k, tn), BF16), pltpu.VMEM((k, tn), BF16)],
        compiler_params=_params(("arbitrary", "arbitrary"), blocks,
                                2 * _nbytes((k, tn), BF16) + 3 * _nbytes((tm, tn), F32)),
        name="ffn_gateup",
    )(a, wg, wu)


def _block_diag_ones(n, block, dtype):
    r = lax.broadcasted_iota(jnp.int32, (n, n), 0) // block
    c = lax.broadcasted_iota(jnp.int32, (n, n), 1) // block
    return jnp.where(r == c, 1.0, 0.0).astype(dtype)


def _head_sums(xs):
    x = xs[0] if len(xs) == 1 else jnp.concatenate(xs, axis=0)
    w = x.shape[1]
    gl = UNIT_LANES if w % UNIT_LANES == 0 else LANE
    bd = _block_diag_ones(gl, RW_HEAD, BF16)
    hi = x.astype(BF16)
    lo = (x - hi.astype(F32)).astype(BF16)
    parts = [_dot(hi[:, s:s + gl], bd) + _dot(lo[:, s:s + gl], bd) for s in range(0, w, gl)]
    out = parts[0] if len(parts) == 1 else jnp.concatenate(parts, axis=1)
    n = xs[0].shape[0]
    return [out[i * n:(i + 1) * n] for i in range(len(xs))]


def _rw_prep_kernel(*refs, rw, seg, has_vfirst, n_prompt, t_prompt):
    if has_vfirst:
        (p_ref, prev_ref, vf_ref, mu_ref, w2_ref, w0_ref, a2_ref, a0_ref, g2_ref, kk_ref, ka_ref,
         v2_ref, v0_ref, r_out, w_out, k_out, v_out, a_out, b_out, g_out) = refs
    else:
        (p_ref, prev_ref, mu_ref, w2_ref, w0_ref, a2_ref, a0_ref, g2_ref, kk_ref, ka_ref,
         r_out, w_out, k_out, v_out, a_out, b_out, g_out) = refs
    tr = p_ref.shape[0]
    row0 = pl.program_id(0) * tr
    rows = lax.broadcasted_iota(jnp.int32, (tr, 1), 0)
    grow = rows + row0
    seq_start = jnp.logical_and(grow < n_prompt, grow % t_prompt == 0)

    def shifted_lerp(lo, width):
        p = p_ref[:, lo:lo + width]
        above = pltpu.roll(p, 1, axis=0)
        above = jnp.where(rows == 0, prev_ref[SUBLANE - 1:SUBLANE, lo:lo + width], above)
        above = jnp.where(seq_start, 0.0, above)
        return p + mu_ref[:, lo:lo + width] * (above - p)

    o_r, o_k, o_v, o_w, w_w, o_a, w_a, o_g, w_g, o_xv, w_xv = seg
    r = shifted_lerp(o_r, rw)
    kr = shifted_lerp(o_k, rw)
    vr = shifted_lerp(o_v, rw)
    xw = shifted_lerp(o_w, w_w)
    xa = shifted_lerp(o_a, w_a)
    xg = shifted_lerp(o_g, w_g)

    w_log = -_softplus(-(w0_ref[...] + _dot(jnp.tanh(xw).astype(BF16), w2_ref[...]))) - 0.5
    decay = jnp.exp(-jnp.exp(w_log))
    a = _sigmoid(a0_ref[...] + _dot(xa.astype(BF16), a2_ref[...]))
    g = _dot(_sigmoid(xg).astype(BF16), g2_ref[...])
    if has_vfirst:
        xv = shifted_lerp(o_xv, w_xv)
        vr = vr + (vf_ref[...] - vr) * _sigmoid(v0_ref[...] + _dot(xv.astype(BF16), v2_ref[...]))
    kk = kr * kk_ref[...]
    kk = kk * lax.rsqrt(jnp.maximum(_head_sums([kk * kk])[0], 1e-24))
    r_out[...] = r
    w_out[...] = decay
    k_out[...] = kr * (1.0 + (a - 1.0) * ka_ref[...])
    v_out[...] = vr
    a_out[...] = -kk
    b_out[...] = kk * a
    g_out[...] = g


def rw_prep(p_rw, seg, rw, mu, w2, w0, a2, a0, g2, kk, ka, v2, v0, v_first, n_prompt, t_prompt):
    m, n_rw = p_rw.shape
    tr = _tile(m, 128, SUBLANE)
    nsub = tr // SUBLANE
    has_vfirst = v_first is not None
    row = lambda w: pl.BlockSpec((1, w), lambda i: (0, 0))
    full = lambda x: pl.BlockSpec(x.shape, lambda i: (0, 0))
    tile = pl.BlockSpec((tr, rw), lambda i: (i, 0))
    in_specs = [pl.BlockSpec((tr, n_rw), lambda i: (i, 0)),
                pl.BlockSpec((SUBLANE, n_rw), lambda i: (jnp.maximum(i * nsub - 1, 0), 0))]
    args = [p_rw, p_rw]
    if has_vfirst:
        in_specs.append(tile)
        args.append(v_first)
    in_specs += [row(n_rw), full(w2), row(rw), full(a2), row(rw), full(g2), row(rw), row(rw)]
    args += [mu.reshape(1, n_rw), w2, w0.reshape(1, rw), a2, a0.reshape(1, rw), g2,
             kk.reshape(1, rw), ka.reshape(1, rw)]
    if has_vfirst:
        in_specs += [full(v2), row(rw)]
        args += [v2, v0.reshape(1, rw)]
    blocks = [_nbytes((tr + SUBLANE, n_rw), F32), (7 + has_vfirst) * _nbytes((tr, rw), F32),
              sum(_nbytes(x.shape, x.dtype) for x in (w2, a2, g2)) + (_nbytes(v2.shape, v2.dtype) if has_vfirst else 0)]
    outs = pl.pallas_call(
        functools.partial(_rw_prep_kernel, rw=rw, seg=seg, has_vfirst=has_vfirst,
                          n_prompt=n_prompt, t_prompt=t_prompt),
        grid=(m // tr,),
        in_specs=in_specs,
        out_specs=[tile] * 7,
        out_shape=[jax.ShapeDtypeStruct((m, rw), F32)] * 7,
        compiler_params=_params(("arbitrary",), blocks, 12 * _nbytes((tr, rw), F32)),
        name="rwkv_prep",
    )(*args)
    return outs


def _rw_scan_kernel(*refs, nb, n_groups, tc, t_start, split_in, layer, owns_stack):
    n_op = 7 * (nb if split_in else 1)
    op_refs = refs[:n_op]
    gnw_ref, gnb_ref, rk_ref, s0_ref = refs[n_op:n_op + 4]
    o_ref, sout_ref, s_scr, lhs_scr, y8_scr, y_scr = refs[-6:]
    c = pl.program_id(1)
    units = [(s, grp) for s in range(nb) for grp in range(n_groups)]

    def operand(kind, s):
        return (op_refs[kind * nb + s], 0) if split_in else (op_refs[kind], s * tc)

    @pl.when(c == 0)
    def _():
        for u, (s, grp) in enumerate(units):
            s_scr[u * RW_HEAD:(u + 1) * RW_HEAD, :] = jnp.concatenate(
                [s0_ref[s, UNIT_HEADS * grp + h] for h in range(UNIT_HEADS)], axis=1)

    bd = _block_diag_ones(UNIT_LANES, RW_HEAD, BF16)
    diag = (lax.broadcasted_iota(jnp.int32, (RW_HEAD, UNIT_LANES), 1) % RW_HEAD
            == lax.broadcasted_iota(jnp.int32, (RW_HEAD, UNIT_LANES), 0))

    def group_step(base, first):
        def row(kind, u, j):
            s, grp = units[u]
            ref, off = operand(kind, s)
            return ref[pl.ds(off + base, SUBLANE), grp * UNIT_LANES:(grp + 1) * UNIT_LANES][j:j + 1, :]

        n_u = len(units)
        q0 = 2 * n_u * RW_HEAD

        def put_sa_v(par, u, s, j):
            lhs_scr[par, 2 * u * RW_HEAD:(2 * u + 1) * RW_HEAD, :] = (s * row(4, u, j)).astype(BF16)
            lhs_scr[par, (2 * u + 1) * RW_HEAD:(2 * u + 2) * RW_HEAD, :] = (
                jnp.where(diag, row(3, u, j), 0.0).astype(BF16))

        if first:
            y8_scr[...] = jnp.zeros(y8_scr.shape, F32)
        for u in range(n_u):
            put_sa_v(0, u, s_scr[u * RW_HEAD:(u + 1) * RW_HEAD, :], first)
        bc = _dot(lhs_scr[0, :q0], bd)
        for j in range(first, SUBLANE):
            par = (j - first + 1) % 2
            last = j == SUBLANE - 1
            for u in range(n_u):
                rows = slice(u * RW_HEAD, (u + 1) * RW_HEAD)
                s = (s_scr[rows, :] * row(1, u, j)
                     + bc[2 * u * RW_HEAD:(2 * u + 1) * RW_HEAD] * row(5, u, j)
                     + bc[(2 * u + 1) * RW_HEAD:(2 * u + 2) * RW_HEAD] * row(2, u, j))
                s_scr[rows, :] = s
                lhs_scr[par, q0 + u * RW_HEAD:q0 + (u + 1) * RW_HEAD, :] = (s * row(0, u, j)).astype(BF16)
                if not last:
                    put_sa_v(par, u, s, j + 1)
            bc = _dot(lhs_scr[par, q0:] if last else lhs_scr[par], bd)
            y_bc = bc if last else bc[q0:]
            for u in range(n_u):
                y_row = jnp.sum(jnp.where(diag, y_bc[u * RW_HEAD:(u + 1) * RW_HEAD], 0.0), axis=0, keepdims=True)
                y8_scr[u * SUBLANE + j:u * SUBLANE + j + 1, :] = y_row
        for u, (s, grp) in enumerate(units):
            y_scr[pl.ds(s * tc + base, SUBLANE), grp * UNIT_LANES:(grp + 1) * UNIT_LANES] = (
                y8_scr[u * SUBLANE:(u + 1) * SUBLANE, :])

    if tc == SUBLANE:
        group_step(0, t_start)
    else:
        assert t_start == 0 and tc % SUBLANE == 0

        def body(i, carry):
            group_step(pl.multiple_of(i * SUBLANE, SUBLANE), 0)
            return carry

        lax.fori_loop(0, tc // SUBLANE, body, 0)

    for s in range(nb):
        def full(kind):
            ref, off = operand(kind, s)
            return ref[off:off + tc, :]
        y = y_scr[s * tc:(s + 1) * tc, :]
        mean, rk_sum = _head_sums([y, full(0) * full(2) * rk_ref[...]])
        yc = y - mean * (1.0 / RW_HEAD)
        var = _head_sums([yc * yc])[0] * (1.0 / RW_HEAD)
        yn = yc * lax.rsqrt(var + GN_EPS) * gnw_ref[...] + gnb_ref[...]
        o_ref[s] = ((yn + rk_sum * full(3)) * full(6)).astype(o_ref.dtype)

    @pl.when(c == pl.num_programs(1) - 1)
    def _():
        out = _state_out(sout_ref, layer, owns_stack)
        for u, (s, grp) in enumerate(units):
            st = s_scr[u * RW_HEAD:(u + 1) * RW_HEAD, :]
            for h in range(UNIT_HEADS):
                out[s, UNIT_HEADS * grp + h] = st[:, h * RW_HEAD:(h + 1) * RW_HEAD]


def _state_io(s0, s0_layer, layer, depth, s_prev, nb, n_inputs):
    blk = (None, nb) + s0.shape[2:]
    zeros = (0,) * (s0.ndim - 2)
    in_spec = pl.BlockSpec(blk, lambda i, c: (s0_layer, i) + zeros)
    out_shape = jax.ShapeDtypeStruct((depth,) + s0.shape[1:], s0.dtype)
    if s_prev is None:
        out_spec = pl.BlockSpec((depth, nb) + s0.shape[2:], lambda i, c: (0, i) + zeros)
        return in_spec, out_spec, out_shape, [], [], {}
    out_spec = pl.BlockSpec(blk, lambda i, c: (layer, i) + zeros)
    return in_spec, out_spec, out_shape, [pl.BlockSpec(memory_space=pl.ANY)], [s_prev], {n_inputs: 1}


def _state_out(sout_ref, layer, owns_stack):
    if not owns_stack:
        return sout_ref
    for other in range(sout_ref.shape[0]):
        if other != layer:
            sout_ref[other] = jnp.zeros(sout_ref.shape[1:], sout_ref.dtype)
    return sout_ref.at[layer]


def rw_scan(ops, g, gn_w, gn_b, rk, s0, s0_layer, layer, depth, s_prev, row0, n_seq, t_seq, tc, t_start, nb):
    rw = ops[0].shape[1]
    n_heads = s0.shape[2]
    assert n_heads % UNIT_HEADS == 0
    n_groups = n_heads // UNIT_HEADS
    n_chunks = t_seq // tc
    split_in = n_chunks > 1
    assert n_seq % nb == 0 and (split_in or (row0 % (nb * tc) == 0 and tc == t_seq))
    if split_in:
        base = row0 // tc
        in_specs = [pl.BlockSpec((tc, rw), lambda i, c, s=s: (base + (i * nb + s) * n_chunks + c, 0))
                    for _ in range(7) for s in range(nb)]
        args = [x for x in (*ops, g) for _ in range(nb)]
    else:
        base = row0 // (nb * tc)
        in_specs = [pl.BlockSpec((nb * tc, rw), lambda i, c: (base + i, 0))] * 7
        args = [*ops, g]
    row = pl.BlockSpec((1, rw), lambda i, c: (0, 0))
    st_in, st_out, st_shape, prev_spec, prev_arg, alias = _state_io(
        s0, s0_layer, layer, depth, s_prev, nb, len(args) + 4)
    n_units = nb * n_groups
    scratch = [pltpu.VMEM((n_units * RW_HEAD, UNIT_LANES), F32),
               pltpu.VMEM((2, 3 * n_units * RW_HEAD, UNIT_LANES), BF16),
               pltpu.VMEM((n_units * SUBLANE, UNIT_LANES), F32),
               pltpu.VMEM((nb * tc, rw), F32)]
    blocks = [7 * _nbytes((nb * tc, rw), F32), _nbytes((nb * tc, rw), BF16),
              (1 + depth) * _nbytes((nb * n_heads, RW_HEAD, LANE), F32)]
    o, s_out = pl.pallas_call(
        functools.partial(_rw_scan_kernel, nb=nb, n_groups=n_groups, tc=tc, t_start=t_start, split_in=split_in,
                          layer=layer, owns_stack=s_prev is None),
        grid=(n_seq // nb, n_chunks),
        in_specs=in_specs + [row] * 3 + [st_in] + prev_spec,
        out_specs=[pl.BlockSpec((nb, tc, rw), lambda i, c: (i, c, 0)), st_out],
        out_shape=[jax.ShapeDtypeStruct((n_seq, t_seq, rw), BF16), st_shape],
        input_output_aliases=alias,
        scratch_shapes=scratch,
        compiler_params=_params(("arbitrary", "arbitrary"), blocks,
                                16 * _nbytes((n_units * RW_HEAD, UNIT_LANES), F32) + 8 * _nbytes((tc, rw), F32)),
        name="rwkv_scan",
    )(*args, gn_w.reshape(1, rw), gn_b.reshape(1, rw), rk.reshape(1, rw), s0, *prev_arg)
    return o.reshape(n_seq * t_seq, rw), s_out


def _gla_kernel(*refs, n_heads, dk, dv, t_start, layer, owns_stack):
    q_ref, k_ref, v_ref, go_ref, lr_ref, w2_ref, gb_ref, gn_ref, s0_ref = refs[:9]
    o_ref, sout_ref, s_scr = refs[-3:]
    c_rows = q_ref.shape[0]
    n = pl.program_id(1)

    @pl.when(n == 0)
    def _():
        s_scr[...] = s0_ref[0]

    rows = lax.broadcasted_iota(jnp.int32, (c_rows, 1), 0)
    valid = rows >= t_start
    tri = (lax.broadcasted_iota(jnp.int32, (c_rows, c_rows), 0)
           >= lax.broadcasted_iota(jnp.int32, (c_rows, c_rows), 1))
    tri_f = jnp.where(tri, 1.0, 0.0).astype(F32)
    lr = lr_ref[...]
    mid = max(c_rows // 2 - 1, 0)
    scale = dk ** -0.5
    for h in range(n_heads):
        ks = slice(h * dk, (h + 1) * dk)
        vs = slice(h * dv, (h + 1) * dv)
        x = _dot(lr, w2_ref[:, ks], precision=HIGHEST) + gb_ref[:, ks]
        log_a = (jnp.minimum(x, 0.0) - jnp.log(1.0 + jnp.exp(-jnp.abs(x)))) * (1.0 / GLA_GATE_TAU)
        log_a = jnp.where(valid, log_a, 0.0)
        b = _dot(tri_f, log_a, precision=HIGHEST)
        b_last = b[c_rows - 1:c_rows, :]
        b_mid = b[mid:mid + 1, :]
        q = q_ref[:, ks] * scale
        k = jnp.where(valid, k_ref[:, ks], 0.0)
        v = v_ref[:, vs].astype(BF16)
        qi = (q * jnp.exp(b - b_mid)).astype(BF16)
        ki = (k * jnp.exp(b_mid - b)).astype(BF16)
        att = lax.dot_general(qi, ki, (((1,), (1,)), ((), ())), preferred_element_type=F32)
        att = jnp.where(tri, att, 0.0).astype(BF16)
        s = s_scr[h]
        o = _dot(att, v) + _dot((q * jnp.exp(b)).astype(BF16), s.astype(BF16))
        kd = (k * jnp.exp(b_last - b)).astype(BF16)
        upd = lax.dot_general(kd, v, (((0,), (0,)), ((), ())), preferred_element_type=F32)
        dcol = jnp.transpose(jnp.broadcast_to(jnp.exp(b_last), (LANE, dk)))
        s_scr[h] = s * jnp.concatenate([dcol] * (dv // LANE), axis=1) + upd
        on = o * lax.rsqrt(jnp.mean(o * o, axis=-1, keepdims=True) + HEAD_NORM_EPS)
        go = go_ref[:, vs]
        o_ref[:, vs] = (on * gn_ref[:, vs] * (go * _sigmoid(go))).astype(o_ref.dtype)

    @pl.when(n == pl.num_programs(1) - 1)
    def _():
        _state_out(sout_ref, layer, owns_stack)[0] = s_scr[...]


def gla(p_gla, p_rw, lr_block, w2p, gb, gn, s0, s0_layer, layer, depth, s_prev, row0, n_seq, t_seq, c_rows, t_start):
    _, _, n_heads, dk, dv = s0.shape
    qk, gw = n_heads * dk, n_heads * dv
    n_chunks = t_seq // c_rows
    base = row0 // c_rows
    rmap = lambda col: (lambda i, n: (base + i * n_chunks + n, col))
    st_in, st_out, st_shape, prev_spec, prev_arg, alias = _state_io(s0, s0_layer, layer, depth, s_prev, 1, 9)
    const = lambda shape: pl.BlockSpec(shape, lambda i, n: (0, 0))
    blocks = [_nbytes((c_rows, 2 * qk + 2 * gw + LANE), F32), _nbytes((LANE + 2, qk), F32),
              _nbytes((c_rows, gw), BF16), (1 + depth) * _nbytes((n_heads, dk, dv), F32)]
    o, s_out = pl.pallas_call(
        functools.partial(_gla_kernel, n_heads=n_heads, dk=dk, dv=dv, t_start=t_start,
                          layer=layer, owns_stack=s_prev is None),
        grid=(n_seq, n_chunks),
        in_specs=[pl.BlockSpec((c_rows, qk), rmap(0)),
                  pl.BlockSpec((c_rows, qk), rmap(1)),
                  pl.BlockSpec((c_rows, gw), rmap(2 * qk // gw)),
                  pl.BlockSpec((c_rows, gw), rmap(2 * qk // gw + 1)),
                  pl.BlockSpec((c_rows, LANE), rmap(lr_block)),
                  const((LANE, qk)), const((1, qk)), const((1, gw)), st_in] + prev_spec,
        out_specs=[pl.BlockSpec((c_rows, gw), lambda i, n: (i * n_chunks + n, 0)), st_out],
        out_shape=[jax.ShapeDtypeStruct((n_seq * t_seq, gw), BF16), st_shape],
        input_output_aliases=alias,
        scratch_shapes=[pltpu.VMEM((n_heads, dk, dv), F32)],
        compiler_params=_params(("arbitrary", "arbitrary"), blocks,
                                _nbytes((n_heads, dk, dv), F32) + 6 * _nbytes((dk, dv), F32)),
        name="gla",
    )(p_gla, p_gla, p_gla, p_gla, p_rw, w2p, gb.reshape(1, qk), gn.reshape(1, gw), s0, *prev_arg)
    return o, s_out


def _router_kernel(x_ref, g_ref, wr_ref, h_ref, info_ref, *, n_experts):
    x = x_ref[...]
    h = x * lax.rsqrt(jnp.mean(x * x, axis=-1, keepdims=True) + NORM_EPS) * g_ref[...]
    half = h.shape[1] // 2
    lo = lax.bitcast_convert_type(h[:, :half].astype(BF16).astype(F32), jnp.uint32)
    hi = lax.bitcast_convert_type(h[:, half:].astype(BF16).astype(F32), jnp.uint32)
    h_ref[...] = (lo >> 16) | (hi & jnp.uint32(0xFFFF0000))
    logits = _dot(h, wr_ref[...], precision=HIGHEST)
    lane = lax.broadcasted_iota(jnp.int32, logits.shape, 1)
    neg = jnp.float32(-jnp.inf)
    logits = jnp.where(lane < n_experts, logits, neg)
    m1 = jnp.max(logits, axis=-1, keepdims=True)
    i1 = jnp.min(jnp.where(logits == m1, lane, LANE), axis=-1, keepdims=True)
    rest = jnp.where(lane == i1, neg, logits)
    m2 = jnp.max(rest, axis=-1, keepdims=True)
    i2 = jnp.min(jnp.where(rest == m2, lane, LANE), axis=-1, keepdims=True)
    e2 = jnp.exp(m2 - m1)
    g1 = 1.0 / (1.0 + e2)
    g2 = e2 / (1.0 + e2)
    info = jnp.where(lane == 0, i1.astype(F32),
                     jnp.where(lane == 1, i2.astype(F32),
                               jnp.where(lane == 2, g1, jnp.where(lane == 3, g2, 0.0))))
    info_ref[...] = info


def moe_router(x, g, router):
    m, d = x.shape
    n_experts = router.shape[1]
    tr = _tile(m, 256, SUBLANE)
    wr = jnp.zeros((d, LANE), F32).at[:, :n_experts].set(router)
    blocks = [_nbytes((tr, d), F32), _nbytes((d, LANE), F32), _nbytes((tr, d // 2), jnp.uint32)]
    return pl.pallas_call(
        functools.partial(_router_kernel, n_experts=n_experts),
        grid=(m // tr,),
        in_specs=[pl.BlockSpec((tr, d), lambda i: (i, 0)), pl.BlockSpec((1, d), lambda i: (0, 0)),
                  pl.BlockSpec((d, LANE), lambda i: (0, 0))],
        out_specs=[pl.BlockSpec((tr, d // 2), lambda i: (i, 0)), pl.BlockSpec((tr, LANE), lambda i: (i, 0))],
        out_shape=[jax.ShapeDtypeStruct((m, d // 2), jnp.uint32), jax.ShapeDtypeStruct((m, LANE), F32)],
        compiler_params=_params(("arbitrary",), blocks, 4 * _nbytes((tr, d), F32)),
        name="moe_router",
    )(x, g.reshape(1, d), wr)


def _gather_kernel(idx_ref, src_ref, out_ref, buf, sem):
    tg, w = buf.shape

    def copy(r):
        return pltpu.make_async_copy(src_ref.at[pl.ds(idx_ref[0, 0, r], 1)], buf.at[pl.ds(r, 1)], sem)

    def start(r, carry):
        copy(r).start()
        return carry

    def wait(r, carry):
        copy(r).wait()
        return carry

    lax.fori_loop(0, tg, start, 0)
    lax.fori_loop(0, tg, wait, 0)
    words = buf[...]
    out_ref[:, :w] = lax.bitcast_convert_type(words << 16, F32).astype(BF16)
    out_ref[:, w:] = lax.bitcast_convert_type(words & jnp.uint32(0xFFFF0000), F32).astype(BF16)


def gather_rows(src, idx, tg=256):
    n = idx.shape[0]
    tg = _tile(n, tg, 16)
    w = src.shape[1]
    return pl.pallas_call(
        _gather_kernel,
        grid=(n // tg,),
        in_specs=[pl.BlockSpec((1, 1, tg), lambda i: (i, 0, 0), memory_space=pltpu.SMEM),
                  pl.BlockSpec(memory_space=pl.ANY)],
        out_specs=pl.BlockSpec((tg, 2 * w), lambda i: (i, 0)),
        out_shape=jax.ShapeDtypeStruct((n, 2 * w), BF16),
        scratch_shapes=[pltpu.VMEM((tg, w), src.dtype), pltpu.SemaphoreType.DMA(())],
        compiler_params=_params(("arbitrary",), [_nbytes((tg, 2 * w), BF16)], 4 * _nbytes((tg, w), src.dtype)),
        name="moe_gather",
    )(idx.reshape(n // tg, 1, tg), src)


def _moe_gateup_kernel(te_ref, tf_ref, tv_ref, a_ref, wg_ref, wu_ref, o_ref, wgb_ref, wub_ref):
    t = pl.program_id(1)

    @pl.when(tf_ref[t] == 1)
    def _():
        wgb_ref[...] = wg_ref[...].astype(BF16)
        wub_ref[...] = wu_ref[...].astype(BF16)

    @pl.when(tv_ref[t] == 1)
    def _():
        a = a_ref[...]
        g = _dot(a, wgb_ref[...])
        u = _dot(a, wub_ref[...])
        o_ref[...] = (g * _sigmoid(g) * u).astype(o_ref.dtype)

    @pl.when(tv_ref[t] == 0)
    def _():
        o_ref[...] = jnp.zeros(o_ref.shape, o_ref.dtype)


def moe_gateup(xs, wg, wu, tile_expert, tile_first, tile_valid, tm, tn_pref=512):
    r, d = xs.shape
    f = wg.shape[2]
    tn = _tile(f, tn_pref, LANE)
    wspec = pl.BlockSpec((None, d, tn), lambda j, t, te, tf, tv: (te[t], 0, j))
    blocks = [_nbytes((tm, d), BF16), 2 * _nbytes((d, tn), F32), _nbytes((tm, tn), BF16)]
    return pl.pallas_call(
        _moe_gateup_kernel,
        grid_spec=pltpu.PrefetchScalarGridSpec(
            num_scalar_prefetch=3,
            grid=(f // tn, r // tm),
            in_specs=[pl.BlockSpec((tm, d), lambda j, t, te, tf, tv: (t, 0)), wspec, wspec],
            out_specs=pl.BlockSpec((tm, tn), lambda j, t, te, tf, tv: (t, j)),
            scratch_shapes=[pltpu.VMEM((d, tn), BF16), pltpu.VMEM((d, tn), BF16)]),
        out_shape=jax.ShapeDtypeStruct((r, f), BF16),
        compiler_params=_params(("arbitrary", "arbitrary"), blocks,
                                2 * _nbytes((d, tn), BF16) + 3 * _nbytes((tm, tn), F32)),
        name="moe_gateup",
    )(tile_expert, tile_first, tile_valid, xs, wg, wu)


def _moe_down_kernel(te_ref, tf_ref, tv_ref, a_ref, w_ref, o_ref, wb_ref):
    t = pl.program_id(1)
    k = pl.program_id(2)

    @pl.when(tf_ref[t] == 1)
    def _():
        wb_ref[k] = w_ref[...].astype(BF16)

    @pl.when(k == 0)
    def _():
        o_ref[...] = jnp.zeros(o_ref.shape, o_ref.dtype)

    @pl.when(tv_ref[t] == 1)
    def _():
        o_ref[...] += _dot(a_ref[...], wb_ref[k])


def moe_down(act, wd, tile_expert, tile_first, tile_valid, tm, tn_pref=512, tk_pref=3584):
    r, f = act.shape
    d = wd.shape[2]
    tn = _tile(d, tn_pref, LANE)
    tk = _tile(f, tk_pref, LANE)
    nk = f // tk

    def wmap(j, t, k, te, tf, tv):
        return (te[t], jnp.where(tf[t] == 1, k, nk - 1), j)

    blocks = [_nbytes((tm, tk), BF16), _nbytes((tk, tn), F32), _nbytes((tm, tn), F32)]
    return pl.pallas_call(
        _moe_down_kernel,
        grid_spec=pltpu.PrefetchScalarGridSpec(
            num_scalar_prefetch=3,
            grid=(d // tn, r // tm, nk),
            in_specs=[pl.BlockSpec((tm, tk), lambda j, t, k, te, tf, tv: (t, k)),
                      pl.BlockSpec((None, tk, tn), wmap)],
            out_specs=pl.BlockSpec((tm, tn), lambda j, t, k, te, tf, tv: (t, j)),
            scratch_shapes=[pltpu.VMEM((nk, tk, tn), BF16)]),
        out_shape=jax.ShapeDtypeStruct((r, d), F32),
        compiler_params=_params(("arbitrary", "arbitrary", "arbitrary"), blocks,
                                _nbytes((f, tn), BF16) + _nbytes((tm, tn), F32)),
        name="moe_down",
    )(tile_expert, tile_first, tile_valid, act, wd)


def _combine_kernel(p1_ref, p2_ref, x_ref, info_ref, ys_ref, o_ref, buf1, buf2, sem):
    tc = x_ref.shape[0]

    def copies(r):
        return (pltpu.make_async_copy(ys_ref.at[pl.ds(p1_ref[0, 0, r], 1)], buf1.at[pl.ds(r, 1)], sem.at[0]),
                pltpu.make_async_copy(ys_ref.at[pl.ds(p2_ref[0, 0, r], 1)], buf2.at[pl.ds(r, 1)], sem.at[1]))

    def start(r, carry):
        c1, c2 = copies(r)
        c1.start()
        c2.start()
        return carry

    def wait(r, carry):
        c1, c2 = copies(r)
        c1.wait()
        c2.wait()
        return carry

    lax.fori_loop(0, tc, start, 0)
    lax.fori_loop(0, tc, wait, 0)
    info = info_ref[...]
    o_ref[...] = x_ref[...] + info[:, 2:3] * buf1[...] + info[:, 3:4] * buf2[...]


def moe_combine(x, info, ys, pos1, pos2, tc=128):
    m, d = x.shape
    tc = _tile(m, tc, SUBLANE)
    idx = pl.BlockSpec((1, 1, tc), lambda i: (i, 0, 0), memory_space=pltpu.SMEM)
    blocks = [2 * _nbytes((tc, d), F32), _nbytes((tc, LANE), F32)]
    return pl.pallas_call(
        _combine_kernel,
        grid=(m // tc,),
        in_specs=[idx, idx, pl.BlockSpec((tc, d), lambda i: (i, 0)), pl.BlockSpec((tc, LANE), lambda i: (i, 0)),
                  pl.BlockSpec(memory_space=pl.ANY)],
        out_specs=pl.BlockSpec((tc, d), lambda i: (i, 0)),
        out_shape=jax.ShapeDtypeStruct((m, d), F32),
        scratch_shapes=[pltpu.VMEM((tc, d), F32), pltpu.VMEM((tc, d), F32), pltpu.SemaphoreType.DMA((2,))],
        compiler_params=_params(("arbitrary",), blocks, 4 * _nbytes((tc, d), F32)),
        name="moe_combine",
    )(pos1.reshape(m // tc, 1, tc), pos2.reshape(m // tc, 1, tc), x, info, ys)


def moe_ffn(x, norm_g, router, wg, wu, wd, row_valid, n_valid, tm=512):
    m, d = x.shape
    n_experts = router.shape[1]
    h_packed, info = moe_router(x, norm_g, router)

    none = n_experts
    e1 = jnp.where(row_valid, info[:, 0].astype(jnp.int32), none)
    e2 = jnp.where(row_valid, info[:, 1].astype(jnp.int32), none)
    e_all = jnp.concatenate([e1, e2])
    onehot = (e_all[:, None] == jnp.arange(n_experts, dtype=jnp.int32)[None, :]).astype(jnp.int32)
    counts = jnp.sum(onehot, axis=0)
    rank = jnp.sum((jnp.cumsum(onehot, axis=0) - onehot) * onehot, axis=1)
    padded = (counts + tm - 1) // tm * tm
    ends = jnp.cumsum(padded)
    starts = ends - padded
    n_rows = _round_up(2 * n_valid + n_experts * (tm - 1), tm)
    n_tiles = n_rows // tm
    pos = jnp.where(e_all < none, jnp.take(starts, jnp.minimum(e_all, none - 1)) + rank, n_rows)
    tok = jnp.tile(jnp.arange(m, dtype=jnp.int32), 2)
    row_src = jnp.zeros((n_rows,), jnp.int32).at[pos].set(tok, mode="drop")
    tile_start = jnp.arange(n_tiles, dtype=jnp.int32) * tm
    tile_valid = (tile_start < ends[-1]).astype(jnp.int32)
    last_tile = jnp.maximum(ends[-1] // tm - 1, 0)
    tile_expert = jnp.searchsorted(ends, jnp.minimum(tile_start, last_tile * tm), side="right").astype(jnp.int32)
    tile_expert = jnp.minimum(tile_expert, n_experts - 1)
    prev_expert = jnp.concatenate([jnp.full((1,), -1, jnp.int32), tile_expert[:-1]])
    tile_first = jnp.logical_and(tile_expert != prev_expert, tile_valid == 1).astype(jnp.int32)
    tile_first = tile_first.at[0].set(1)
    pos_safe = jnp.where(pos < n_rows, pos, 0)
    gates = jnp.where(row_valid[:, None], info, 0.0)

    xs = gather_rows(h_packed, row_src)
    act = moe_gateup(xs, wg, wu, tile_expert, tile_first, tile_valid, tm)
    ys = moe_down(act, wd, tile_expert, tile_first, tile_valid, tm)
    return moe_combine(x, gates, ys, pos_safe[:m], pos_safe[m:])


def _pad_cols(w, width):
    return w if w.shape[-1] == width else jnp.pad(w, [(0, 0)] * (w.ndim - 1) + [(0, width - w.shape[-1])])


def _pad_rows(w, rows):
    return w if w.shape[0] == rows else jnp.pad(w, [(0, rows - w.shape[0])] + [(0, 0)] * (w.ndim - 1))


def kernel(x_prompt, x_sample, state_gla, state_rwkv, state_shift, norm_mix, norm_ffn, norm_final,
           w_in_first, w_in_rest, mu_first, mu_rest, gla_w2, gla_b, gla_norm,
           rw_w2, rw_w0, rw_a2, rw_a0, rw_g2, rw_kk, rw_ka, rw_rk, rw_gn_w, rw_gn_b, rw_v2, rw_v0,
           w_out, ffn_w_gate, ffn_w_up, ffn_w_down, moe_router_w, moe_w_gate, moe_w_up, moe_w_down):
    batch, seq, d = x_prompt.shape
    dec_batch, dec_seq, _ = x_sample.shape
    depth = norm_mix.shape[0]
    _, _, gla_heads, dk, dv = state_gla.shape
    rw_heads = state_rwkv.shape[2]
    assert state_rwkv.shape[3] == RW_HEAD and state_rwkv.shape[4] == RW_HEAD and rw_heads % 2 == 0
    qk, gw, rw = gla_heads * dk, gla_heads * dv, rw_heads * RW_HEAD
    rank = gla_w2.shape[1]
    p_gla = 2 * qk + 2 * gw + rank
    assert dk % LANE == 0 and dv % LANE == 0 and (2 * qk) % gw == 0 and rank <= LANE

    group = _round_up(dec_seq + 1, SUBLANE)
    t_start = group - dec_seq
    n_prompt = batch * seq
    m = n_prompt + dec_batch * group
    prompt_chunk = _tile(seq, 64, SUBLANE)
    scan_chunk = _tile(seq, 64, SUBLANE)
    sample_rows = jnp.arange(dec_batch * group, dtype=jnp.int32) % group >= t_start
    row_valid = jnp.concatenate([jnp.ones((n_prompt,), bool), sample_rows])

    x = jnp.concatenate([
        x_prompt.reshape(n_prompt, d),
        jnp.concatenate([jnp.zeros((dec_batch, t_start, d), F32), x_sample], axis=1).reshape(dec_batch * group, d)])

    lw, la, lg = rw_w2.shape[1], rw_a2.shape[1], rw_g2.shape[1]
    lv = rw_v2.shape[1]
    pw, pa, pg, pv = (_round_up(n, LANE) for n in (lw, la, lg, lv))
    o_w = 3 * rw
    o_a = o_w + pw
    o_g = o_a + pa
    o_xv = o_g + pg
    o_lr = o_xv + pv
    n_rw = _round_up(o_lr + LANE, 512)
    seg = (0, rw, 2 * rw, o_w, pw, o_a, pa, o_g, pg, o_xv, pv)

    def repack(w_in_l, mu_l, has_v):
        wr = w_in_l[:, p_gla:]
        mu_l = mu_l
        parts_w, parts_mu = [], []
        src = 0
        for width, padded in ((3 * rw, 3 * rw), (lw, pw), (la, pa), (lg, pg)) + (((lv, pv),) if has_v else ()):
            parts_w.append(_pad_cols(wr[:, src:src + width], padded))
            parts_mu.append(_pad_cols(mu_l[src:src + width], padded))
            src += width
        if not has_v:
            parts_w.append(jnp.zeros((d, pv), F32))
            parts_mu.append(jnp.zeros((pv,), F32))
        parts_w.append(_pad_cols(w_in_l[:, p_gla - rank:p_gla], n_rw - o_lr))
        parts_mu.append(jnp.zeros((n_rw - o_lr,), F32))
        return jnp.concatenate(parts_w, axis=1).astype(BF16), jnp.concatenate(parts_mu)

    zeros_gla = jnp.zeros((1, batch, gla_heads, dk, dv), F32)
    zeros_rw = jnp.zeros((1, batch, rw_heads, RW_HEAD, RW_HEAD), F32)
    last_rows = jnp.concatenate([jnp.arange(batch, dtype=jnp.int32) * seq + seq - 1,
                                 n_prompt + jnp.arange(dec_batch, dtype=jnp.int32) * group + group - 1])
    n_last = _round_up(batch + dec_batch, SUBLANE)
    prev_rows = n_prompt + jnp.arange(dec_batch, dtype=jnp.int32) * group + t_start - 1

    new_gla_p = new_gla_s = new_rw_p = new_rw_s = None
    new_shift = []
    v_first = None
    for l in range(depth):
        h = rmsnorm(x, norm_mix[l], BF16).at[prev_rows].set(state_shift[l].astype(BF16))
        x_last = _pad_rows(jnp.take(x, last_rows, axis=0), n_last)
        new_shift.append(rmsnorm(x_last, norm_mix[l], F32)[:batch + dec_batch])

        if l == 0:
            w_in_l, mu_l = w_in_first, mu_first
        else:
            w_in_l, mu_l = w_in_rest[l - 1], mu_rest[l - 1]
        w_rw, mu_p = repack(w_in_l, mu_l, l > 0)
        pg_all = matmul([h], w_in_l, 2 * qk + 2 * gw, name="w_in_gla")
        pr_all = matmul([h], w_rw, n_rw, name="w_in_rwkv")

        ops = rw_prep(pr_all, seg, rw, mu_p,
                      _pad_rows(rw_w2[l], pw).astype(BF16), rw_w0[l], _pad_rows(rw_a2[l], pa).astype(BF16), rw_a0[l],
                      _pad_rows(rw_g2[l], pg).astype(BF16), rw_kk[l], rw_ka[l],
                      _pad_rows(rw_v2[l - 1], pv).astype(BF16) if l > 0 else None,
                      rw_v0[l - 1] if l > 0 else None, v_first, n_prompt, seq)
        r_, w_, k_, v_, a_, b_, g_ = ops
        if l == 0:
            v_first = v_
        scan_args = ((r_, w_, k_, v_, a_, b_), g_, rw_gn_w[l], rw_gn_b[l], rw_rk[l].reshape(rw))
        o_rw_p, new_rw_p = rw_scan(*scan_args, zeros_rw, 0, l, depth, new_rw_p,
                                   0, batch, seq, scan_chunk, 0, _tile(batch, 2, 1))
        o_rw_s, new_rw_s = rw_scan(*scan_args, state_rwkv, l, l, depth, new_rw_s,
                                   n_prompt, dec_batch, group, group, t_start, _tile(dec_batch, 4, 1))

        w2p = _pad_rows(gla_w2[l], LANE)
        gla_args = (pg_all, pr_all, o_lr // LANE, w2p, gla_b[l], gla_norm[l])
        o_gla_p, new_gla_p = gla(*gla_args, zeros_gla, 0, l, depth, new_gla_p, 0, batch, seq, prompt_chunk, 0)
        o_gla_s, new_gla_s = gla(*gla_args, state_gla, l, l, depth, new_gla_s,
                                 n_prompt, dec_batch, group, group, t_start)

        o_gla = jnp.concatenate([o_gla_p, o_gla_s])
        o_rw = jnp.concatenate([o_rw_p, o_rw_s])
        x = matmul([o_gla, o_rw], w_out, d, res=x, layer=l, name="w_out")

        i = l // 2
        if l % 2 == 0:
            h2 = rmsnorm(x, norm_ffn[l], BF16)
            act = gateup(h2, ffn_w_gate[i], ffn_w_up[i])
            x = matmul([act], ffn_w_down[i].astype(BF16), d, res=x, tm_pref=512, name="ffn_down")
        else:
            x = moe_ffn(x, norm_ffn[l], moe_router_w[i], moe_w_gate[i], moe_w_up[i], moe_w_down[i],
                        row_valid, n_prompt + dec_batch * dec_seq)

    y_prompt = rmsnorm(x, norm_final, F32, 0, n_prompt).reshape(batch, seq, d)
    y_sample = rmsnorm(x, norm_final, F32, n_prompt, m - n_prompt).reshape(dec_batch, group, d)[:, t_start:]
    shift = jnp.stack(new_shift)
    return (y_prompt, y_sample, new_gla_p, new_gla_s, new_rw_p, new_rw_s, shift[:, :batch], shift[:, batch:])
```

```python
import functools
import math

import jax
import jax.numpy as jnp
from jax import lax
from jax.experimental import pallas as pl
from jax.experimental.pallas import tpu as pltpu

F32 = jnp.float32
BF16 = jnp.bfloat16
HIGHEST = lax.Precision.HIGHEST

NORM_EPS = 1e-6
HEAD_NORM_EPS = 1e-5
GN_EPS = 64e-5
GLA_GATE_TAU = 16.0

LANE = 128
SUBLANE = 8
RW_HEAD = 64
MXU_DIM = 256
UNIT_HEADS = MXU_DIM // RW_HEAD
UNIT_LANES = MXU_DIM
VMEM_CAP = 64 << 20
VMEM_BUDGET = VMEM_CAP - (6 << 20)


def _round_up(n, m):
    return (n + m - 1) // m * m


def _tile(n, pref, align):
    t = min(pref, n)
    t -= t % align
    while t >= align:
        if n % t == 0:
            return t
        t -= align
    return n


def _nbytes(shape, dtype):
    return math.prod(shape) * jnp.dtype(dtype).itemsize


def _params(sem, blocks, scratch=0):
    need = 2 * sum(blocks) + scratch
    limit = min(VMEM_BUDGET, max(need + need // 4, 16 << 20))
    return pltpu.CompilerParams(dimension_semantics=sem, vmem_limit_bytes=int(limit))


def _sigmoid(x):
    return 1.0 / (1.0 + jnp.exp(-x))


def _softplus(x):
    return jnp.maximum(x, 0.0) + jnp.log(1.0 + jnp.exp(-jnp.abs(x)))


def _dot(a, b, **kw):
    return jnp.dot(a, b, preferred_element_type=F32, **kw)


def _rmsnorm_kernel(x_ref, g_ref, o_ref):
    x = x_ref[...]
    y = x * lax.rsqrt(jnp.mean(x * x, axis=-1, keepdims=True) + NORM_EPS)
    o_ref[...] = (y * g_ref[...]).astype(o_ref.dtype)


def rmsnorm(x, g, out_dtype, row0=0, rows=None):
    d = x.shape[1]
    m = x.shape[0] if rows is None else rows
    tr = _tile(math.gcd(m, row0) if row0 else m, 256, SUBLANE)
    first = row0 // tr
    return pl.pallas_call(
        _rmsnorm_kernel,
        grid=(m // tr,),
        in_specs=[pl.BlockSpec((tr, d), lambda i: (first + i, 0)), pl.BlockSpec((1, d), lambda i: (0, 0))],
        out_specs=pl.BlockSpec((tr, d), lambda i: (i, 0)),
        out_shape=jax.ShapeDtypeStruct((m, d), out_dtype),
        compiler_params=_params(("arbitrary",), [_nbytes((tr, d), F32), _nbytes((tr, d), out_dtype)]),
        name="rmsnorm",
    )(x, g.reshape(1, d))


def _mm_kernel(*refs, n_a, has_res, cast_w, w_t):
    a_refs = refs[:n_a]
    w_ref = refs[n_a]
    res_ref = refs[n_a + 1] if has_res else None
    o_ref = refs[n_a + 1 + has_res]
    if cast_w:
        wb_ref = refs[n_a + 2 + has_res]

        @pl.when(pl.program_id(1) == 0)
        def _():
            wb_ref[...] = w_ref[...].astype(BF16)
        w_ref = wb_ref
    acc = None
    off = 0
    for a_ref in a_refs:
        ka = a_ref.shape[1]
        if w_t:
            part = lax.dot_general(a_ref[...], w_ref[:, off:off + ka], (((1,), (1,)), ((), ())),
                                   preferred_element_type=F32)
        else:
            part = _dot(a_ref[...], w_ref[off:off + ka, :])
        acc = part if acc is None else acc + part
        off += ka
    if has_res:
        acc = acc + res_ref[...]
    o_ref[...] = acc.astype(o_ref.dtype)


def matmul(a_list, w, n_out, res=None, out_dtype=F32, tm_pref=1024, tn_pref=512, layer=None, w_t=False,
           name="matmul"):
    m = a_list[0].shape[0]
    k = w.shape[-1] if w_t else w.shape[-2]
    assert sum(a.shape[1] for a in a_list) == k and (w.ndim == 2) == (layer is None) and not (w_t and layer)
    tm = _tile(m, tm_pref, 16)
    tn = _tile(n_out, tn_pref, LANE)
    cast_w = w.dtype != BF16
    in_specs = [pl.BlockSpec((tm, a.shape[1]), lambda j, i: (i, 0)) for a in a_list]
    if w_t:
        in_specs.append(pl.BlockSpec((tn, k), lambda j, i: (j, 0)))
    elif layer is None:
        in_specs.append(pl.BlockSpec((k, tn), lambda j, i: (0, j)))
    else:
        in_specs.append(pl.BlockSpec((None, k, tn), lambda j, i: (layer, 0, j)))
    args = list(a_list) + [w]
    blocks = [_nbytes((tm, k), BF16), _nbytes((k, tn), w.dtype), _nbytes((tm, tn), out_dtype)]
    if res is not None:
        in_specs.append(pl.BlockSpec((tm, tn), lambda j, i: (i, j)))
        args.append(res)
        blocks.append(_nbytes((tm, tn), F32))
    scratch = [pltpu.VMEM((tn, k) if w_t else (k, tn), BF16)] if cast_w else []
    return pl.pallas_call(
        functools.partial(_mm_kernel, n_a=len(a_list), has_res=res is not None, cast_w=cast_w, w_t=w_t),
        grid=(n_out // tn, m // tm),
        in_specs=in_specs,
        out_specs=pl.BlockSpec((tm, tn), lambda j, i: (i, j)),
        out_shape=jax.ShapeDtypeStruct((m, n_out), out_dtype),
        scratch_shapes=scratch,
        compiler_params=_params(("arbitrary", "arbitrary"), blocks,
                                _nbytes((k, tn), BF16) * cast_w + 2 * _nbytes((tm, tn), F32)),
        name=name,
    )(*args)


def _gateup_kernel(a_ref, wg_ref, wu_ref, o_ref, wgb_ref, wub_ref):
    @pl.when(pl.program_id(1) == 0)
    def _():
        wgb_ref[...] = wg_ref[...].astype(BF16)
        wub_ref[...] = wu_ref[...].astype(BF16)
    a = a_ref[...]
    g = _dot(a, wgb_ref[...])
    u = _dot(a, wub_ref[...])
    o_ref[...] = (g * _sigmoid(g) * u).astype(o_ref.dtype)


def gateup(a, wg, wu, tm_pref=1024, tn_pref=256):
    m, k = a.shape
    f = wg.shape[1]
    tm = _tile(m, tm_pref, 16)
    tn = _tile(f, tn_pref, LANE)
    blocks = [_nbytes((tm, k), BF16), 2 * _nbytes((k, tn), F32), _nbytes((tm, tn), BF16)]
    return pl.pallas_call(
        _gateup_kernel,
        grid=(f // tn, m // tm),
        in_specs=[pl.BlockSpec((tm, k), lambda j, i: (i, 0)),
                  pl.BlockSpec((k, tn), lambda j, i: (0, j)),
                  pl.BlockSpec((k, tn), lambda j, i: (0, j))],
        out_specs=pl.BlockSpec((tm, tn), lambda j, i: (i, j)),
        out_shape=jax.ShapeDtypeStruct((m, f), BF16),
        scratch_shapes=[pltpu.VMEM((k, tn), BF16), pltpu.VMEM((k, tn), BF16)],
        compiler_params=_params(("arbitrary", "arbitrary"), blocks,
                                2 * _nbytes((k, tn), BF16) + 3 * _nbytes((tm, tn), F32)),
        name="ffn_gateup",
    )(a, wg, wu)


def _block_diag_ones(n, block, dtype):
    r = lax.broadcasted_iota(jnp.int32, (n, n), 0) // block
    c = lax.broadcasted_iota(jnp.int32, (n, n), 1) // block
    return jnp.where(r == c, 1.0, 0.0).astype(dtype)


def _head_sums(xs):
    x = xs[0] if len(xs) == 1 else jnp.concatenate(xs, axis=0)
    w = x.shape[1]
    gl = UNIT_LANES if w % UNIT_LANES == 0 else LANE
    bd = _block_diag_ones(gl, RW_HEAD, BF16)
    hi = x.astype(BF16)
    lo = (x - hi.astype(F32)).astype(BF16)
    parts = [_dot(hi[:, s:s + gl], bd) + _dot(lo[:, s:s + gl], bd) for s in range(0, w, gl)]
    out = parts[0] if len(parts) == 1 else jnp.concatenate(parts, axis=1)
    n = xs[0].shape[0]
    return [out[i * n:(i + 1) * n] for i in range(len(xs))]


def _rw_prep_kernel(*refs, rw, seg, has_vfirst, n_prompt, t_prompt):
    if has_vfirst:
        (p_ref, prev_ref, vf_ref, mu_ref, w2_ref, w0_ref, a2_ref, a0_ref, g2_ref, kk_ref, ka_ref,
         v2_ref, v0_ref, r_out, w_out, k_out, v_out, a_out, b_out, g_out) = refs
    else:
        (p_ref, prev_ref, mu_ref, w2_ref, w0_ref, a2_ref, a0_ref, g2_ref, kk_ref, ka_ref,
         r_out, w_out, k_out, v_out, a_out, b_out, g_out) = refs
    tr = p_ref.shape[0]
    row0 = pl.program_id(0) * tr
    rows = lax.broadcasted_iota(jnp.int32, (tr, 1), 0)
    grow = rows + row0
    seq_start = jnp.logical_and(grow < n_prompt, grow % t_prompt == 0)

    def shifted_lerp(lo, width):
        p = p_ref[:, lo:lo + width]
        above = pltpu.roll(p, 1, axis=0)
        above = jnp.where(rows == 0, prev_ref[SUBLANE - 1:SUBLANE, lo:lo + width], above)
        above = jnp.where(seq_start, 0.0, above)
        return p + mu_ref[:, lo:lo + width] * (above - p)

    o_r, o_k, o_v, o_w, w_w, o_a, w_a, o_g, w_g, o_xv, w_xv = seg
    r = shifted_lerp(o_r, rw)
    kr = shifted_lerp(o_k, rw)
    vr = shifted_lerp(o_v, rw)
    xw = shifted_lerp(o_w, w_w)
    xa = shifted_lerp(o_a, w_a)
    xg = shifted_lerp(o_g, w_g)

    w_log = -_softplus(-(w0_ref[...] + _dot(jnp.tanh(xw).astype(BF16), w2_ref[...]))) - 0.5
    decay = jnp.exp(-jnp.exp(w_log))
    a = _sigmoid(a0_ref[...] + _dot(xa.astype(BF16), a2_ref[...]))
    g = _dot(_sigmoid(xg).astype(BF16), g2_ref[...])
    if has_vfirst:
        xv = shifted_lerp(o_xv, w_xv)
        vr = vr + (vf_ref[...] - vr) * _sigmoid(v0_ref[...] + _dot(xv.astype(BF16), v2_ref[...]))
    kk = kr * kk_ref[...]
    kk = kk * lax.rsqrt(jnp.maximum(_head_sums([kk * kk])[0], 1e-24))
    r_out[...] = r
    w_out[...] = decay
    k_out[...] = kr * (1.0 + (a - 1.0) * ka_ref[...])
    v_out[...] = vr
    a_out[...] = -kk
    b_out[...] = kk * a
    g_out[...] = g


def rw_prep(p_rw, seg, rw, mu, w2, w0, a2, a0, g2, kk, ka, v2, v0, v_first, n_prompt, t_prompt):
    m, n_rw = p_rw.shape
    tr = _tile(m, 128, SUBLANE)
    nsub = tr // SUBLANE
    has_vfirst = v_first is not None
    row = lambda w: pl.BlockSpec((1, w), lambda i: (0, 0))
    full = lambda x: pl.BlockSpec(x.shape, lambda i: (0, 0))
    tile = pl.BlockSpec((tr, rw), lambda i: (i, 0))
    in_specs = [pl.BlockSpec((tr, n_rw), lambda i: (i, 0)),
                pl.BlockSpec((SUBLANE, n_rw), lambda i: (jnp.maximum(i * nsub - 1, 0), 0))]
    args = [p_rw, p_rw]
    if has_vfirst:
        in_specs.append(tile)
        args.append(v_first)
    in_specs += [row(n_rw), full(w2), row(rw), full(a2), row(rw), full(g2), row(rw), row(rw)]
    args += [mu.reshape(1, n_rw), w2, w0.reshape(1, rw), a2, a0.reshape(1, rw), g2,
             kk.reshape(1, rw), ka.reshape(1, rw)]
    if has_vfirst:
        in_specs += [full(v2), row(rw)]
        args += [v2, v0.reshape(1, rw)]
    blocks = [_nbytes((tr + SUBLANE, n_rw), F32), (7 + has_vfirst) * _nbytes((tr, rw), F32),
              sum(_nbytes(x.shape, x.dtype) for x in (w2, a2, g2)) + (_nbytes(v2.shape, v2.dtype) if has_vfirst else 0)]
    outs = pl.pallas_call(
        functools.partial(_rw_prep_kernel, rw=rw, seg=seg, has_vfirst=has_vfirst,
                          n_prompt=n_prompt, t_prompt=t_prompt),
        grid=(m // tr,),
        in_specs=in_specs,
        out_specs=[tile] * 7,
        out_shape=[jax.ShapeDtypeStruct((m, rw), F32)] * 7,
        compiler_params=_params(("arbitrary",), blocks, 12 * _nbytes((tr, rw), F32)),
        name="rwkv_prep",
    )(*args)
    return outs


def _rw_scan_kernel(*refs, nb, n_groups, tc, t_start, split_in, layer, owns_stack):
    n_op = 7 * (nb if split_in else 1)
    op_refs = refs[:n_op]
    gnw_ref, gnb_ref, rk_ref, s0_ref = refs[n_op:n_op + 4]
    o_ref, sout_ref, s_scr, lhs_scr, y8_scr, y_scr = refs[-6:]
    c = pl.program_id(1)
    units = [(s, grp) for s in range(nb) for grp in range(n_groups)]

    def operand(kind, s):
        return (op_refs[kind * nb + s], 0) if split_in else (op_refs[kind], s * tc)

    @pl.when(c == 0)
    def _():
        for u, (s, grp) in enumerate(units):
            s_scr[u * RW_HEAD:(u + 1) * RW_HEAD, :] = jnp.concatenate(
                [s0_ref[s, UNIT_HEADS * grp + h] for h in range(UNIT_HEADS)], axis=1)

    bd = _block_diag_ones(UNIT_LANES, RW_HEAD, BF16)
    diag = (lax.broadcasted_iota(jnp.int32, (RW_HEAD, UNIT_LANES), 1) % RW_HEAD
            == lax.broadcasted_iota(jnp.int32, (RW_HEAD, UNIT_LANES), 0))

    def group_step(base, first):
        def row(kind, u, j):
            s, grp = units[u]
            ref, off = operand(kind, s)
            return ref[pl.ds(off + base, SUBLANE), grp * UNIT_LANES:(grp + 1) * UNIT_LANES][j:j + 1, :]

        n_u = len(units)
        q0 = 2 * n_u * RW_HEAD

        def put_sa_v(par, u, s, j):
            lhs_scr[par, 2 * u * RW_HEAD:(2 * u + 1) * RW_HEAD, :] = (s * row(4, u, j)).astype(BF16)
            lhs_scr[par, (2 * u + 1) * RW_HEAD:(2 * u + 2) * RW_HEAD, :] = (
                jnp.where(diag, row(3, u, j), 0.0).astype(BF16))

        if first:
            y8_scr[...] = jnp.zeros(y8_scr.shape, F32)
        for u in range(n_u):
            put_sa_v(0, u, s_scr[u * RW_HEAD:(u + 1) * RW_HEAD, :], first)
        bc = _dot(lhs_scr[0, :q0], bd)
        for j in range(first, SUBLANE):
            par = (j - first + 1) % 2
            last = j == SUBLANE - 1
            for u in range(n_u):
                rows = slice(u * RW_HEAD, (u + 1) * RW_HEAD)
                s = (s_scr[rows, :] * row(1, u, j)
                     + bc[2 * u * RW_HEAD:(2 * u + 1) * RW_HEAD] * row(5, u, j)
                     + bc[(2 * u + 1) * RW_HEAD:(2 * u + 2) * RW_HEAD] * row(2, u, j))
                s_scr[rows, :] = s
                lhs_scr[par, q0 + u * RW_HEAD:q0 + (u + 1) * RW_HEAD, :] = (s * row(0, u, j)).astype(BF16)
                if not last:
                    put_sa_v(par, u, s, j + 1)
            bc = _dot(lhs_scr[par, q0:] if last else lhs_scr[par], bd)
            y_bc = bc if last else bc[q0:]
            for u in range(n_u):
                y_row = jnp.sum(jnp.where(diag, y_bc[u * RW_HEAD:(u + 1) * RW_HEAD], 0.0), axis=0, keepdims=True)
                y8_scr[u * SUBLANE + j:u * SUBLANE + j + 1, :] = y_row
        for u, (s, grp) in enumerate(units):
            y_scr[pl.ds(s * tc + base, SUBLANE), grp * UNIT_LANES:(grp + 1) * UNIT_LANES] = (
                y8_scr[u * SUBLANE:(u + 1) * SUBLANE, :])

    if tc == SUBLANE:
        group_step(0, t_start)
    else:
        assert t_start == 0 and tc % SUBLANE == 0

        def body(i, carry):
            group_step(pl.multiple_of(i * SUBLANE, SUBLANE), 0)
            return carry

        lax.fori_loop(0, tc // SUBLANE, body, 0)

    for s in range(nb):
        def full(kind):
            ref, off = operand(kind, s)
            return ref[off:off + tc, :]
        y = y_scr[s * tc:(s + 1) * tc, :]
        mean, rk_sum = _head_sums([y, full(0) * full(2) * rk_ref[...]])
        yc = y - mean * (1.0 / RW_HEAD)
        var = _head_sums([yc * yc])[0] * (1.0 / RW_HEAD)
        yn = yc * lax.rsqrt(var + GN_EPS) * gnw_ref[...] + gnb_ref[...]
        o_ref[s] = ((yn + rk_sum * full(3)) * full(6)).astype(o_ref.dtype)

    @pl.when(c == pl.num_programs(1) - 1)
    def _():
        out = _state_out(sout_ref, layer, owns_stack)
        for u, (s, grp) in enumerate(units):
            st = s_scr[u * RW_HEAD:(u + 1) * RW_HEAD, :]
            for h in range(UNIT_HEADS):
                out[s, UNIT_HEADS * grp + h] = st[:, h * RW_HEAD:(h + 1) * RW_HEAD]


def _state_io(s0, s0_layer, layer, depth, s_prev, nb, n_inputs):
    blk = (None, nb) + s0.shape[2:]
    zeros = (0,) * (s0.ndim - 2)
    in_spec = pl.BlockSpec(blk, lambda i, c: (s0_layer, i) + zeros)
    out_shape = jax.ShapeDtypeStruct((depth,) + s0.shape[1:], s0.dtype)
    if s_prev is None:
        out_spec = pl.BlockSpec((depth, nb) + s0.shape[2:], lambda i, c: (0, i) + zeros)
        return in_spec, out_spec, out_shape, [], [], {}
    out_spec = pl.BlockSpec(blk, lambda i, c: (layer, i) + zeros)
    return in_spec, out_spec, out_shape, [pl.BlockSpec(memory_space=pl.ANY)], [s_prev], {n_inputs: 1}


def _state_out(sout_ref, layer, owns_stack):
    if not owns_stack:
        return sout_ref
    for other in range(sout_ref.shape[0]):
        if other != layer:
            sout_ref[other] = jnp.zeros(sout_ref.shape[1:], sout_ref.dtype)
    return sout_ref.at[layer]


def rw_scan(ops, g, gn_w, gn_b, rk, s0, s0_layer, layer, depth, s_prev, row0, n_seq, t_seq, tc, t_start, nb):
    rw = ops[0].shape[1]
    n_heads = s0.shape[2]
    assert n_heads % UNIT_HEADS == 0
    n_groups = n_heads // UNIT_HEADS
    n_chunks = t_seq // tc
    split_in = n_chunks > 1
    assert n_seq % nb == 0 and (split_in or (row0 % (nb * tc) == 0 and tc == t_seq))
    if split_in:
        base = row0 // tc
        in_specs = [pl.BlockSpec((tc, rw), lambda i, c, s=s: (base + (i * nb + s) * n_chunks + c, 0))
                    for _ in range(7) for s in range(nb)]
        args = [x for x in (*ops, g) for _ in range(nb)]
    else:
        base = row0 // (nb * tc)
        in_specs = [pl.BlockSpec((nb * tc, rw), lambda i, c: (base + i, 0))] * 7
        args = [*ops, g]
    row = pl.BlockSpec((1, rw), lambda i, c: (0, 0))
    st_in, st_out, st_shape, prev_spec, prev_arg, alias = _state_io(
        s0, s0_layer, layer, depth, s_prev, nb, len(args) + 4)
    n_units = nb * n_groups
    scratch = [pltpu.VMEM((n_units * RW_HEAD, UNIT_LANES), F32),
               pltpu.VMEM((2, 3 * n_units * RW_HEAD, UNIT_LANES), BF16),
               pltpu.VMEM((n_units * SUBLANE, UNIT_LANES), F32),
               pltpu.VMEM((nb * tc, rw), F32)]
    blocks = [7 * _nbytes((nb * tc, rw), F32), _nbytes((nb * tc, rw), BF16),
              (1 + depth) * _nbytes((nb * n_heads, RW_HEAD, LANE), F32)]
    o, s_out = pl.pallas_call(
        functools.partial(_rw_scan_kernel, nb=nb, n_groups=n_groups, tc=tc, t_start=t_start, split_in=split_in,
                          layer=layer, owns_stack=s_prev is None),
        grid=(n_seq // nb, n_chunks),
        in_specs=in_specs + [row] * 3 + [st_in] + prev_spec,
        out_specs=[pl.BlockSpec((nb, tc, rw), lambda i, c: (i, c, 0)), st_out],
        out_shape=[jax.ShapeDtypeStruct((n_seq, t_seq, rw), BF16), st_shape],
        input_output_aliases=alias,
        scratch_shapes=scratch,
        compiler_params=_params(("arbitrary", "arbitrary"), blocks,
                                16 * _nbytes((n_units * RW_HEAD, UNIT_LANES), F32) + 8 * _nbytes((tc, rw), F32)),
        name="rwkv_scan",
    )(*args, gn_w.reshape(1, rw), gn_b.reshape(1, rw), rk.reshape(1, rw), s0, *prev_arg)
    return o.reshape(n_seq * t_seq, rw), s_out


def _gla_kernel(*refs, n_heads, dk, dv, t_start, layer, owns_stack):
    q_ref, k_ref, v_ref, go_ref, lr_ref, w2_ref, gb_ref, gn_ref, s0_ref = refs[:9]
    o_ref, sout_ref, s_scr = refs[-3:]
    c_rows = q_ref.shape[0]
    n = pl.program_id(1)

    @pl.when(n == 0)
    def _():
        s_scr[...] = s0_ref[0]

    rows = lax.broadcasted_iota(jnp.int32, (c_rows, 1), 0)
    valid = rows >= t_start
    tri = (lax.broadcasted_iota(jnp.int32, (c_rows, c_rows), 0)
           >= lax.broadcasted_iota(jnp.int32, (c_rows, c_rows), 1))
    tri_f = jnp.where(tri, 1.0, 0.0).astype(F32)
    lr = lr_ref[...]
    mid = max(c_rows // 2 - 1, 0)
    scale = dk ** -0.5
    for h in range(n_heads):
        ks = slice(h * dk, (h + 1) * dk)
        vs = slice(h * dv, (h + 1) * dv)
        x = _dot(lr, w2_ref[:, ks], precision=HIGHEST) + gb_ref[:, ks]
        log_a = (jnp.minimum(x, 0.0) - jnp.log(1.0 + jnp.exp(-jnp.abs(x)))) * (1.0 / GLA_GATE_TAU)
        log_a = jnp.where(valid, log_a, 0.0)
        b = _dot(tri_f, log_a, precision=HIGHEST)
        b_last = b[c_rows - 1:c_rows, :]
        b_mid = b[mid:mid + 1, :]
        q = q_ref[:, ks] * scale
        k = jnp.where(valid, k_ref[:, ks], 0.0)
        v = v_ref[:, vs].astype(BF16)
        qi = (q * jnp.exp(b - b_mid)).astype(BF16)
        ki = (k * jnp.exp(b_mid - b)).astype(BF16)
        att = lax.dot_general(qi, ki, (((1,), (1,)), ((), ())), preferred_element_type=F32)
        att = jnp.where(tri, att, 0.0).astype(BF16)
        s = s_scr[h]
        o = _dot(att, v) + _dot((q * jnp.exp(b)).astype(BF16), s.astype(BF16))
        kd = (k * jnp.exp(b_last - b)).astype(BF16)
        upd = lax.dot_general(kd, v, (((0,), (0,)), ((), ())), preferred_element_type=F32)
        dcol = jnp.transpose(jnp.broadcast_to(jnp.exp(b_last), (LANE, dk)))
        s_scr[h] = s * jnp.concatenate([dcol] * (dv // LANE), axis=1) + upd
        on = o * lax.rsqrt(jnp.mean(o * o, axis=-1, keepdims=True) + HEAD_NORM_EPS)
        go = go_ref[:, vs]
        o_ref[:, vs] = (on * gn_ref[:, vs] * (go * _sigmoid(go))).astype(o_ref.dtype)

    @pl.when(n == pl.num_programs(1) - 1)
    def _():
        _state_out(sout_ref, layer, owns_stack)[0] = s_scr[...]


def gla(p_gla, p_rw, lr_block, w2p, gb, gn, s0, s0_layer, layer, depth, s_prev, row0, n_seq, t_seq, c_rows, t_start):
    _, _, n_heads, dk, dv = s0.shape
    qk, gw = n_heads * dk, n_heads * dv
    n_chunks = t_seq // c_rows
    base = row0 // c_rows
    rmap = lambda col: (lambda i, n: (base + i * n_chunks + n, col))
    st_in, st_out, st_shape, prev_spec, prev_arg, alias = _state_io(s0, s0_layer, layer, depth, s_prev, 1, 9)
    const = lambda shape: pl.BlockSpec(shape, lambda i, n: (0, 0))
    blocks = [_nbytes((c_rows, 2 * qk + 2 * gw + LANE), F32), _nbytes((LANE + 2, qk), F32),
              _nbytes((c_rows, gw), BF16), (1 + depth) * _nbytes((n_heads, dk, dv), F32)]
    o, s_out = pl.pallas_call(
        functools.partial(_gla_kernel, n_heads=n_heads, dk=dk, dv=dv, t_start=t_start,
                          layer=layer, owns_stack=s_prev is None),
        grid=(n_seq, n_chunks),
        in_specs=[pl.BlockSpec((c_rows, qk), rmap(0)),
                  pl.BlockSpec((c_rows, qk), rmap(1)),
                  pl.BlockSpec((c_rows, gw), rmap(2 * qk // gw)),
                  pl.BlockSpec((c_rows, gw), rmap(2 * qk // gw + 1)),
                  pl.BlockSpec((c_rows, LANE), rmap(lr_block)),
                  const((LANE, qk)), const((1, qk)), const((1, gw)), st_in] + prev_spec,
        out_specs=[pl.BlockSpec((c_rows, gw), lambda i, n: (i * n_chunks + n, 0)), st_out],
        out_shape=[jax.ShapeDtypeStruct((n_seq * t_seq, gw), BF16), st_shape],
        input_output_aliases=alias,
        scratch_shapes=[pltpu.VMEM((n_heads, dk, dv), F32)],
        compiler_params=_params(("arbitrary", "arbitrary"), blocks,
                                _nbytes((n_heads, dk, dv), F32) + 6 * _nbytes((dk, dv), F32)),
        name="gla",
    )(p_gla, p_gla, p_gla, p_gla, p_rw, w2p, gb.reshape(1, qk), gn.reshape(1, gw), s0, *prev_arg)
    return o, s_out


def _router_kernel(x_ref, g_ref, wr_ref, h_ref, info_ref, *, n_experts):
    x = x_ref[...]
    h = x * lax.rsqrt(jnp.mean(x * x, axis=-1, keepdims=True) + NORM_EPS) * g_ref[...]
    half = h.shape[1] // 2
    lo = lax.bitcast_convert_type(h[:, :half].astype(BF16).astype(F32), jnp.uint32)
    hi = lax.bitcast_convert_type(h[:, half:].astype(BF16).astype(F32), jnp.uint32)
    h_ref[...] = (lo >> 16) | (hi & jnp.uint32(0xFFFF0000))
    logits = _dot(h, wr_ref[...], precision=HIGHEST)
    lane = lax.broadcasted_iota(jnp.int32, logits.shape, 1)
    neg = jnp.float32(-jnp.inf)
    logits = jnp.where(lane < n_experts, logits, neg)
    m1 = jnp.max(logits, axis=-1, keepdims=True)
    i1 = jnp.min(jnp.where(logits == m1, lane, LANE), axis=-1, keepdims=True)
    rest = jnp.where(lane == i1, neg, logits)
    m2 = jnp.max(rest, axis=-1, keepdims=True)
    i2 = jnp.min(jnp.where(rest == m2, lane, LANE), axis=-1, keepdims=True)
    e2 = jnp.exp(m2 - m1)
    g1 = 1.0 / (1.0 + e2)
    g2 = e2 / (1.0 + e2)
    info = jnp.where(lane == 0, i1.astype(F32),
                     jnp.where(lane == 1, i2.astype(F32),
                               jnp.where(lane == 2, g1, jnp.where(lane == 3, g2, 0.0))))
    info_ref[...] = info


def moe_router(x, g, router):
    m, d = x.shape
    n_experts = router.shape[1]
    tr = _tile(m, 256, SUBLANE)
    wr = jnp.zeros((d, LANE), F32).at[:, :n_experts].set(router)
    blocks = [_nbytes((tr, d), F32), _nbytes((d, LANE), F32), _nbytes((tr, d // 2), jnp.uint32)]
    return pl.pallas_call(
        functools.partial(_router_kernel, n_experts=n_experts),
        grid=(m // tr,),
        in_specs=[pl.BlockSpec((tr, d), lambda i: (i, 0)), pl.BlockSpec((1, d), lambda i: (0, 0)),
                  pl.BlockSpec((d, LANE), lambda i: (0, 0))],
        out_specs=[pl.BlockSpec((tr, d // 2), lambda i: (i, 0)), pl.BlockSpec((tr, LANE), lambda i: (i, 0))],
        out_shape=[jax.ShapeDtypeStruct((m, d // 2), jnp.uint32), jax.ShapeDtypeStruct((m, LANE), F32)],
        compiler_params=_params(("arbitrary",), blocks, 4 * _nbytes((tr, d), F32)),
        name="moe_router",
    )(x, g.reshape(1, d), wr)


def _gather_kernel(idx_ref, src_ref, out_ref, buf, sem):
    tg, w = buf.shape

    def copy(r):
        return pltpu.make_async_copy(src_ref.at[pl.ds(idx_ref[0, 0, r], 1)], buf.at[pl.ds(r, 1)], sem)

    def start(r, carry):
        copy(r).start()
        return carry

    def wait(r, carry):
        copy(r).wait()
        return carry

    lax.fori_loop(0, tg, start, 0)
    lax.fori_loop(0, tg, wait, 0)
    words = buf[...]
    out_ref[:, :w] = lax.bitcast_convert_type(words << 16, F32).astype(BF16)
    out_ref[:, w:] = lax.bitcast_convert_type(words & jnp.uint32(0xFFFF0000), F32).astype(BF16)


def gather_rows(src, idx, tg=256):
    n = idx.shape[0]
    tg = _tile(n, tg, 16)
    w = src.shape[1]
    return pl.pallas_call(
        _gather_kernel,
        grid=(n // tg,),
        in_specs=[pl.BlockSpec((1, 1, tg), lambda i: (i, 0, 0), memory_space=pltpu.SMEM),
                  pl.BlockSpec(memory_space=pl.ANY)],
        out_specs=pl.BlockSpec((tg, 2 * w), lambda i: (i, 0)),
        out_shape=jax.ShapeDtypeStruct((n, 2 * w), BF16),
        scratch_shapes=[pltpu.VMEM((tg, w), src.dtype), pltpu.SemaphoreType.DMA(())],
        compiler_params=_params(("arbitrary",), [_nbytes((tg, 2 * w), BF16)], 4 * _nbytes((tg, w), src.dtype)),
        name="moe_gather",
    )(idx.reshape(n // tg, 1, tg), src)


def _moe_gateup_kernel(te_ref, tf_ref, tv_ref, a_ref, wg_ref, wu_ref, o_ref, wgb_ref, wub_ref):
    t = pl.program_id(1)

    @pl.when(tf_ref[t] == 1)
    def _():
        wgb_ref[...] = wg_ref[...].astype(BF16)
        wub_ref[...] = wu_ref[...].astype(BF16)

    @pl.when(tv_ref[t] == 1)
    def _():
        a = a_ref[...]
        g = _dot(a, wgb_ref[...])
        u = _dot(a, wub_ref[...])
        o_ref[...] = (g * _sigmoid(g) * u).astype(o_ref.dtype)

    @pl.when(tv_ref[t] == 0)
    def _():
        o_ref[...] = jnp.zeros(o_ref.shape, o_ref.dtype)


def moe_gateup(xs, wg, wu, tile_expert, tile_first, tile_valid, tm, tn_pref=512):
    r, d = xs.shape
    f = wg.shape[2]
    tn = _tile(f, tn_pref, LANE)
    wspec = pl.BlockSpec((None, d, tn), lambda j, t, te, tf, tv: (te[t], 0, j))
    blocks = [_nbytes((tm, d), BF16), 2 * _nbytes((d, tn), F32), _nbytes((tm, tn), BF16)]
    return pl.pallas_call(
        _moe_gateup_kernel,
        grid_spec=pltpu.PrefetchScalarGridSpec(
            num_scalar_prefetch=3,
            grid=(f // tn, r // tm),
            in_specs=[pl.BlockSpec((tm, d), lambda j, t, te, tf, tv: (t, 0)), wspec, wspec],
            out_specs=pl.BlockSpec((tm, tn), lambda j, t, te, tf, tv: (t, j)),
            scratch_shapes=[pltpu.VMEM((d, tn), BF16), pltpu.VMEM((d, tn), BF16)]),
        out_shape=jax.ShapeDtypeStruct((r, f), BF16),
        compiler_params=_params(("arbitrary", "arbitrary"), blocks,
                                2 * _nbytes((d, tn), BF16) + 3 * _nbytes((tm, tn), F32)),
        name="moe_gateup",
    )(tile_expert, tile_first, tile_valid, xs, wg, wu)


def _moe_down_kernel(te_ref, tf_ref, tv_ref, a_ref, w_ref, o_ref, wb_ref):
    t = pl.program_id(1)
    k = pl.program_id(2)

    @pl.when(tf_ref[t] == 1)
    def _():
        wb_ref[k] = w_ref[...].astype(BF16)

    @pl.when(k == 0)
    def _():
        o_ref[...] = jnp.zeros(o_ref.shape, o_ref.dtype)

    @pl.when(tv_ref[t] == 1)
    def _():
        o_ref[...] += _dot(a_ref[...], wb_ref[k])


def moe_down(act, wd, tile_expert, tile_first, tile_valid, tm, tn_pref=512, tk_pref=3584):
    r, f = act.shape
    d = wd.shape[2]
    tn = _tile(d, tn_pref, LANE)
    tk = _tile(f, tk_pref, LANE)
    nk = f // tk

    def wmap(j, t, k, te, tf, tv):
        return (te[t], jnp.where(tf[t] == 1, k, nk - 1), j)

    blocks = [_nbytes((tm, tk), BF16), _nbytes((tk, tn), F32), _nbytes((tm, tn), F32)]
    return pl.pallas_call(
        _moe_down_kernel,
        grid_spec=pltpu.PrefetchScalarGridSpec(
            num_scalar_prefetch=3,
            grid=(d // tn, r // tm, nk),
            in_specs=[pl.BlockSpec((tm, tk), lambda j, t, k, te, tf, tv: (t, k)),
                      pl.BlockSpec((None, tk, tn), wmap)],
            out_specs=pl.BlockSpec((tm, tn), lambda j, t, k, te, tf, tv: (t, j)),
            scratch_shapes=[pltpu.VMEM((nk, tk, tn), BF16)]),
        out_shape=jax.ShapeDtypeStruct((r, d), F32),
        compiler_params=_params(("arbitrary", "arbitrary", "arbitrary"), blocks,
                                _nbytes((f, tn), BF16) + _nbytes((tm, tn), F32)),
        name="moe_down",
    )(tile_expert, tile_first, tile_valid, act, wd)


def _combine_kernel(p1_ref, p2_ref, x_ref, info_ref, ys_ref, o_ref, buf1, buf2, sem):
    tc = x_ref.shape[0]

    def copies(r):
        return (pltpu.make_async_copy(ys_ref.at[pl.ds(p1_ref[0, 0, r], 1)], buf1.at[pl.ds(r, 1)], sem.at[0]),
                pltpu.make_async_copy(ys_ref.at[pl.ds(p2_ref[0, 0, r], 1)], buf2.at[pl.ds(r, 1)], sem.at[1]))

    def start(r, carry):
        c1, c2 = copies(r)
        c1.start()
        c2.start()
        return carry

    def wait(r, carry):
        c1, c2 = copies(r)
        c1.wait()
        c2.wait()
        return carry

    lax.fori_loop(0, tc, start, 0)
    lax.fori_loop(0, tc, wait, 0)
    info = info_ref[...]
    o_ref[...] = x_ref[...] + info[:, 2:3] * buf1[...] + info[:, 3:4] * buf2[...]


def moe_combine(x, info, ys, pos1, pos2, tc=128):
    m, d = x.shape
    tc = _tile(m, tc, SUBLANE)
    idx = pl.BlockSpec((1, 1, tc), lambda i: (i, 0, 0), memory_space=pltpu.SMEM)
    blocks = [2 * _nbytes((tc, d), F32), _nbytes((tc, LANE), F32)]
    return pl.pallas_call(
        _combine_kernel,
        grid=(m // tc,),
        in_specs=[idx, idx, pl.BlockSpec((tc, d), lambda i: (i, 0)), pl.BlockSpec((tc, LANE), lambda i: (i, 0)),
                  pl.BlockSpec(memory_space=pl.ANY)],
        out_specs=pl.BlockSpec((tc, d), lambda i: (i, 0)),
        out_shape=jax.ShapeDtypeStruct((m, d), F32),
        scratch_shapes=[pltpu.VMEM((tc, d), F32), pltpu.VMEM((tc, d), F32), pltpu.SemaphoreType.DMA((2,))],
        compiler_params=_params(("arbitrary",), blocks, 4 * _nbytes((tc, d), F32)),
        name="moe_combine",
    )(pos1.reshape(m // tc, 1, tc), pos2.reshape(m // tc, 1, tc), x, info, ys)


def moe_ffn(x, norm_g, router, wg, wu, wd, row_valid, n_valid, tm=384):
    m, d = x.shape
    n_experts = router.shape[1]
    h_packed, info = moe_router(x, norm_g, router)

    none = n_experts
    e1 = jnp.where(row_valid, info[:, 0].astype(jnp.int32), none)
    e2 = jnp.where(row_valid, info[:, 1].astype(jnp.int32), none)
    e_all = jnp.concatenate([e1, e2])
    onehot = (e_all[:, None] == jnp.arange(n_experts, dtype=jnp.int32)[None, :]).astype(jnp.int32)
    counts = jnp.sum(onehot, axis=0)
    rank = jnp.sum((jnp.cumsum(onehot, axis=0) - onehot) * onehot, axis=1)
    padded = (counts + tm - 1) // tm * tm
    ends = jnp.cumsum(padded)
    starts = ends - padded
    n_rows = _round_up(2 * n_valid + n_experts * (tm - 1), tm)
    n_tiles = n_rows // tm
    pos = jnp.where(e_all < none, jnp.take(starts, jnp.minimum(e_all, none - 1)) + rank, n_rows)
    tok = jnp.tile(jnp.arange(m, dtype=jnp.int32), 2)
    row_src = jnp.zeros((n_rows,), jnp.int32).at[pos].set(tok, mode="drop")
    tile_start = jnp.arange(n_tiles, dtype=jnp.int32) * tm
    tile_valid = (tile_start < ends[-1]).astype(jnp.int32)
    last_tile = jnp.maximum(ends[-1] // tm - 1, 0)
    tile_expert = jnp.searchsorted(ends, jnp.minimum(tile_start, last_tile * tm), side="right").astype(jnp.int32)
    tile_expert = jnp.minimum(tile_expert, n_experts - 1)
    prev_expert = jnp.concatenate([jnp.full((1,), -1, jnp.int32), tile_expert[:-1]])
    tile_first = jnp.logical_and(tile_expert != prev_expert, tile_valid == 1).astype(jnp.int32)
    tile_first = tile_first.at[0].set(1)
    pos_safe = jnp.where(pos < n_rows, pos, 0)
    gates = jnp.where(row_valid[:, None], info, 0.0)

    xs = gather_rows(h_packed, row_src)
    act = moe_gateup(xs, wg, wu, tile_expert, tile_first, tile_valid, tm)
    ys = moe_down(act, wd, tile_expert, tile_first, tile_valid, tm)
    return moe_combine(x, gates, ys, pos_safe[:m], pos_safe[m:])


def _pad_cols(w, width):
    return w if w.shape[-1] == width else jnp.pad(w, [(0, 0)] * (w.ndim - 1) + [(0, width - w.shape[-1])])


def _pad_rows(w, rows):
    return w if w.shape[0] == rows else jnp.pad(w, [(0, rows - w.shape[0])] + [(0, 0)] * (w.ndim - 1))


def kernel(x_prompt, x_sample, state_gla, state_rwkv, state_shift, norm_mix, norm_ffn, norm_final,
           w_in_first, w_in_rest, mu_first, mu_rest, gla_w2, gla_b, gla_norm,
           rw_w2, rw_w0, rw_a2, rw_a0, rw_g2, rw_kk, rw_ka, rw_rk, rw_gn_w, rw_gn_b, rw_v2, rw_v0,
           w_out, ffn_w_gate, ffn_w_up, ffn_w_down, moe_router_w, moe_w_gate, moe_w_up, moe_w_down):
    batch, seq, d = x_prompt.shape
    dec_batch, dec_seq, _ = x_sample.shape
    depth = norm_mix.shape[0]
    _, _, gla_heads, dk, dv = state_gla.shape
    rw_heads = state_rwkv.shape[2]
    assert state_rwkv.shape[3] == RW_HEAD and state_rwkv.shape[4] == RW_HEAD and rw_heads % 2 == 0
    qk, gw, rw = gla_heads * dk, gla_heads * dv, rw_heads * RW_HEAD
    rank = gla_w2.shape[1]
    p_gla = 2 * qk + 2 * gw + rank
    assert dk % LANE == 0 and dv % LANE == 0 and (2 * qk) % gw == 0 and rank <= LANE

    group = _round_up(dec_seq + 1, SUBLANE)
    t_start = group - dec_seq
    n_prompt = batch * seq
    m = n_prompt + dec_batch * group
    prompt_chunk = _tile(seq, 64, SUBLANE)
    scan_chunk = _tile(seq, 64, SUBLANE)
    sample_rows = jnp.arange(dec_batch * group, dtype=jnp.int32) % group >= t_start
    row_valid = jnp.concatenate([jnp.ones((n_prompt,), bool), sample_rows])

    x = jnp.concatenate([
        x_prompt.reshape(n_prompt, d),
        jnp.concatenate([jnp.zeros((dec_batch, t_start, d), F32), x_sample], axis=1).reshape(dec_batch * group, d)])

    lw, la, lg = rw_w2.shape[1], rw_a2.shape[1], rw_g2.shape[1]
    lv = rw_v2.shape[1]
    pw, pa, pg, pv = (_round_up(n, LANE) for n in (lw, la, lg, lv))
    o_w = 3 * rw
    o_a = o_w + pw
    o_g = o_a + pa
    o_xv = o_g + pg
    o_lr = o_xv + pv
    n_rw = _round_up(o_lr + LANE, 512)
    seg = (0, rw, 2 * rw, o_w, pw, o_a, pa, o_g, pg, o_xv, pv)

    def repack(wt, mu_l, has_v):
        wr = wt[p_gla:]
        parts_w, parts_mu = [], []
        src = 0
        for width, padded in ((3 * rw, 3 * rw), (lw, pw), (la, pa), (lg, pg)) + (((lv, pv),) if has_v else ()):
            parts_w.append(_pad_rows(wr[src:src + width], padded))
            parts_mu.append(_pad_cols(mu_l[src:src + width], padded))
            src += width
        if not has_v:
            parts_w.append(jnp.zeros((pv, d), F32))
            parts_mu.append(jnp.zeros((pv,), F32))
        parts_w.append(_pad_rows(wt[p_gla - rank:p_gla], n_rw - o_lr))
        parts_mu.append(jnp.zeros((n_rw - o_lr,), F32))
        return jnp.concatenate(parts_w, axis=0).astype(BF16), jnp.concatenate(parts_mu)

    zeros_gla = jnp.zeros((1, batch, gla_heads, dk, dv), F32)
    zeros_rw = jnp.zeros((1, batch, rw_heads, RW_HEAD, RW_HEAD), F32)
    last_rows = jnp.concatenate([jnp.arange(batch, dtype=jnp.int32) * seq + seq - 1,
                                 n_prompt + jnp.arange(dec_batch, dtype=jnp.int32) * group + group - 1])
    n_last = _round_up(batch + dec_batch, SUBLANE)
    prev_rows = n_prompt + jnp.arange(dec_batch, dtype=jnp.int32) * group + t_start - 1

    new_gla_p = new_gla_s = new_rw_p = new_rw_s = None
    new_shift = []
    v_first = None
    for l in range(depth):
        h = rmsnorm(x, norm_mix[l], BF16).at[prev_rows].set(state_shift[l].astype(BF16))
        x_last = _pad_rows(jnp.take(x, last_rows, axis=0), n_last)
        new_shift.append(rmsnorm(x_last, norm_mix[l], F32)[:batch + dec_batch])

        if l == 0:
            wt, mu_l = w_in_first.T, mu_first
        else:
            wt, mu_l = w_in_rest[l - 1].T, mu_rest[l - 1]
        w_rw, mu_p = repack(wt, mu_l, l > 0)
        pg_all = matmul([h], wt, 2 * qk + 2 * gw, w_t=True, name="w_in_gla")
        pr_all = matmul([h], w_rw, n_rw, w_t=True, name="w_in_rwkv")

        ops = rw_prep(pr_all, seg, rw, mu_p,
                      _pad_rows(rw_w2[l], pw).astype(BF16), rw_w0[l], _pad_rows(rw_a2[l], pa).astype(BF16), rw_a0[l],
                      _pad_rows(rw_g2[l], pg).astype(BF16), rw_kk[l], rw_ka[l],
                      _pad_rows(rw_v2[l - 1], pv).astype(BF16) if l > 0 else None,
                      rw_v0[l - 1] if l > 0 else None, v_first, n_prompt, seq)
        r_, w_, k_, v_, a_, b_, g_ = ops
        if l == 0:
            v_first = v_
        scan_args = ((r_, w_, k_, v_, a_, b_), g_, rw_gn_w[l], rw_gn_b[l], rw_rk[l].reshape(rw))
        o_rw_p, new_rw_p = rw_scan(*scan_args, zeros_rw, 0, l, depth, new_rw_p,
                                   0, batch, seq, scan_chunk, 0, _tile(batch, 4, 1))
        o_rw_s, new_rw_s = rw_scan(*scan_args, state_rwkv, l, l, depth, new_rw_s,
                                   n_prompt, dec_batch, group, group, t_start, _tile(dec_batch, 4, 1))

        w2p = _pad_rows(gla_w2[l], LANE)
        gla_args = (pg_all, pr_all, o_lr // LANE, w2p, gla_b[l], gla_norm[l])
        o_gla_p, new_gla_p = gla(*gla_args, zeros_gla, 0, l, depth, new_gla_p, 0, batch, seq, prompt_chunk, 0)
        o_gla_s, new_gla_s = gla(*gla_args, state_gla, l, l, depth, new_gla_s,
                                 n_prompt, dec_batch, group, group, t_start)

        o_gla = jnp.concatenate([o_gla_p, o_gla_s])
        o_rw = jnp.concatenate([o_rw_p, o_rw_s])
        x = matmul([o_gla, o_rw], w_out, d, res=x, layer=l, name="w_out")

        i = l // 2
        if l % 2 == 0:
            h2 = rmsnorm(x, norm_ffn[l], BF16)
            act = gateup(h2, ffn_w_gate[i], ffn_w_up[i])
            x = matmul([act], ffn_w_down[i].astype(BF16), d, res=x, tm_pref=512, name="ffn_down")
        else:
            x = moe_ffn(x, norm_ffn[l], moe_router_w[i], moe_w_gate[i], moe_w_up[i], moe_w_down[i],
                        row_valid, n_prompt + dec_batch * dec_seq)

    y_prompt = rmsnorm(x, norm_final, F32, 0, n_prompt).reshape(batch, seq, d)
    y_sample = rmsnorm(x, norm_final, F32, n_prompt, m - n_prompt).reshape(dec_batch, group, d)[:, t_start:]
    shift = jnp.stack(new_shift)
    return (y_prompt, y_sample, new_gla_p, new_gla_s, new_rw_p, new_rw_s, shift[:, :batch], shift[:, batch:])
```

```python
import functools
import math

import jax
import jax.numpy as jnp
from jax import lax
from jax.experimental import pallas as pl
from jax.experimental.pallas import tpu as pltpu

F32 = jnp.float32
BF16 = jnp.bfloat16
HIGHEST = lax.Precision.HIGHEST

NORM_EPS = 1e-6
HEAD_NORM_EPS = 1e-5
GN_EPS = 64e-5
GLA_GATE_TAU = 16.0

LANE = 128
SUBLANE = 8
RW_HEAD = 64
MXU_DIM = 256
UNIT_HEADS = MXU_DIM // RW_HEAD
UNIT_LANES = MXU_DIM
VMEM_CAP = 64 << 20
VMEM_BUDGET = VMEM_CAP - (6 << 20)


def _round_up(n, m):
    return (n + m - 1) // m * m


def _tile(n, pref, align):
    t = min(pref, n)
    t -= t % align
    while t >= align:
        if n % t == 0:
            return t
        t -= align
    return n


def _nbytes(shape, dtype):
    return math.prod(shape) * jnp.dtype(dtype).itemsize


def _params(sem, blocks, scratch=0):
    need = 2 * sum(blocks) + scratch
    limit = min(VMEM_BUDGET, max(need + need // 4, 16 << 20))
    return pltpu.CompilerParams(dimension_semantics=sem, vmem_limit_bytes=int(limit))


def _sigmoid(x):
    return 1.0 / (1.0 + jnp.exp(-x))


def _softplus(x):
    return jnp.maximum(x, 0.0) + jnp.log(1.0 + jnp.exp(-jnp.abs(x)))


def _dot(a, b, **kw):
    return jnp.dot(a, b, preferred_element_type=F32, **kw)


def _rmsnorm_kernel(x_ref, g_ref, o_ref):
    x = x_ref[...]
    y = x * lax.rsqrt(jnp.mean(x * x, axis=-1, keepdims=True) + NORM_EPS)
    o_ref[...] = (y * g_ref[...]).astype(o_ref.dtype)


def rmsnorm(x, g, out_dtype, row0=0, rows=None):
    d = x.shape[1]
    m = x.shape[0] if rows is None else rows
    tr = _tile(math.gcd(m, row0) if row0 else m, 256, SUBLANE)
    first = row0 // tr
    return pl.pallas_call(
        _rmsnorm_kernel,
        grid=(m // tr,),
        in_specs=[pl.BlockSpec((tr, d), lambda i: (first + i, 0)), pl.BlockSpec((1, d), lambda i: (0, 0))],
        out_specs=pl.BlockSpec((tr, d), lambda i: (i, 0)),
        out_shape=jax.ShapeDtypeStruct((m, d), out_dtype),
        compiler_params=_params(("arbitrary",), [_nbytes((tr, d), F32), _nbytes((tr, d), out_dtype)]),
        name="rmsnorm",
    )(x, g.reshape(1, d))


def _mm_kernel(*refs, n_a, has_res, cast_w, w_t):
    a_refs = refs[:n_a]
    w_ref = refs[n_a]
    res_ref = refs[n_a + 1] if has_res else None
    o_ref = refs[n_a + 1 + has_res]
    if cast_w:
        wb_ref = refs[n_a + 2 + has_res]

        @pl.when(pl.program_id(1) == 0)
        def _():
            wb_ref[...] = w_ref[...].astype(BF16)
        w_ref = wb_ref
    acc = None
    off = 0
    for a_ref in a_refs:
        ka = a_ref.shape[1]
        if w_t:
            part = lax.dot_general(a_ref[...], w_ref[:, off:off + ka], (((1,), (1,)), ((), ())),
                                   preferred_element_type=F32)
        else:
            part = _dot(a_ref[...], w_ref[off:off + ka, :])
        acc = part if acc is None else acc + part
        off += ka
    if has_res:
        acc = acc + res_ref[...]
    o_ref[...] = acc.astype(o_ref.dtype)


def matmul(a_list, w, n_out, res=None, out_dtype=F32, tm_pref=1024, tn_pref=512, layer=None, w_t=False,
           name="matmul"):
    m = a_list[0].shape[0]
    k = w.shape[-1] if w_t else w.shape[-2]
    assert sum(a.shape[1] for a in a_list) == k and (w.ndim == 2) == (layer is None) and not (w_t and layer)
    tm = _tile(m, tm_pref, 16)
    tn = _tile(n_out, tn_pref, LANE)
    cast_w = w.dtype != BF16
    in_specs = [pl.BlockSpec((tm, a.shape[1]), lambda j, i: (i, 0)) for a in a_list]
    if w_t:
        in_specs.append(pl.BlockSpec((tn, k), lambda j, i: (j, 0)))
    elif layer is None:
        in_specs.append(pl.BlockSpec((k, tn), lambda j, i: (0, j)))
    else:
        in_specs.append(pl.BlockSpec((None, k, tn), lambda j, i: (layer, 0, j)))
    args = list(a_list) + [w]
    blocks = [_nbytes((tm, k), BF16), _nbytes((k, tn), w.dtype), _nbytes((tm, tn), out_dtype)]
    if res is not None:
        in_specs.append(pl.BlockSpec((tm, tn), lambda j, i: (i, j)))
        args.append(res)
        blocks.append(_nbytes((tm, tn), F32))
    scratch = [pltpu.VMEM((tn, k) if w_t else (k, tn), BF16)] if cast_w else []
    return pl.pallas_call(
        functools.partial(_mm_kernel, n_a=len(a_list), has_res=res is not None, cast_w=cast_w, w_t=w_t),
        grid=(n_out // tn, m // tm),
        in_specs=in_specs,
        out_specs=pl.BlockSpec((tm, tn), lambda j, i: (i, j)),
        out_shape=jax.ShapeDtypeStruct((m, n_out), out_dtype),
        scratch_shapes=scratch,
        compiler_params=_params(("arbitrary", "arbitrary"), blocks,
                                _nbytes((k, tn), BF16) * cast_w + 2 * _nbytes((tm, tn), F32)),
        name=name,
    )(*args)


def _gateup_kernel(a_ref, wg_ref, wu_ref, o_ref, wgb_ref, wub_ref):
    @pl.when(pl.program_id(1) == 0)
    def _():
        wgb_ref[...] = wg_ref[...].astype(BF16)
        wub_ref[...] = wu_ref[...].astype(BF16)
    a = a_ref[...]
    g = _dot(a, wgb_ref[...])
    u = _dot(a, wub_ref[...])
    o_ref[...] = (g * _sigmoid(g) * u).astype(o_ref.dtype)


def gateup(a, wg, wu, tm_pref=1024, tn_pref=256):
    m, k = a.shape
    f = wg.shape[1]
    tm = _tile(m, tm_pref, 16)
    tn = _tile(f, tn_pref, LANE)
    blocks = [_nbytes((tm, k), BF16), 2 * _nbytes((k, tn), F32), _nbytes((tm, tn), BF16)]
    return pl.pallas_call(
        _gateup_kernel,
        grid=(f // tn, m // tm),
        in_specs=[pl.BlockSpec((tm, k), lambda j, i: (i, 0)),
                  pl.BlockSpec((k, tn), lambda j, i: (0, j)),
                  pl.BlockSpec((k, tn), lambda j, i: (0, j))],
        out_specs=pl.BlockSpec((tm, tn), lambda j, i: (i, j)),
        out_shape=jax.ShapeDtypeStruct((m, f), BF16),
        scratch_shapes=[pltpu.VMEM((k, tn), BF16), pltpu.VMEM((k, tn), BF16)],
        compiler_params=_params(("arbitrary", "arbitrary"), blocks,
                                2 * _nbytes((k, tn), BF16) + 3 * _nbytes((tm, tn), F32)),
        name="ffn_gateup",
    )(a, wg, wu)


def _block_diag_ones(n, block, dtype):
    r = lax.broadcasted_iota(jnp.int32, (n, n), 0) // block
    c = lax.broadcasted_iota(jnp.int32, (n, n), 1) // block
    return jnp.where(r == c, 1.0, 0.0).astype(dtype)


def _head_sums(xs):
    x = xs[0] if len(xs) == 1 else jnp.concatenate(xs, axis=0)
    w = x.shape[1]
    gl = UNIT_LANES if w % UNIT_LANES == 0 else LANE
    bd = _block_diag_ones(gl, RW_HEAD, BF16)
    hi = x.astype(BF16)
    lo = (x - hi.astype(F32)).astype(BF16)
    parts = [_dot(hi[:, s:s + gl], bd) + _dot(lo[:, s:s + gl], bd) for s in range(0, w, gl)]
    out = parts[0] if len(parts) == 1 else jnp.concatenate(parts, axis=1)
    n = xs[0].shape[0]
    return [out[i * n:(i + 1) * n] for i in range(len(xs))]


def _rw_prep_kernel(*refs, rw, seg, has_vfirst, n_prompt, t_prompt):
    if has_vfirst:
        (p_ref, prev_ref, vf_ref, mu_ref, w2_ref, w0_ref, a2_ref, a0_ref, g2_ref, kk_ref, ka_ref,
         v2_ref, v0_ref, r_out, w_out, k_out, v_out, a_out, b_out, g_out) = refs
    else:
        (p_ref, prev_ref, mu_ref, w2_ref, w0_ref, a2_ref, a0_ref, g2_ref, kk_ref, ka_ref,
         r_out, w_out, k_out, v_out, a_out, b_out, g_out) = refs
    tr = p_ref.shape[0]
    row0 = pl.program_id(0) * tr
    rows = lax.broadcasted_iota(jnp.int32, (tr, 1), 0)
    grow = rows + row0
    seq_start = jnp.logical_and(grow < n_prompt, grow % t_prompt == 0)

    def shifted_lerp(lo, width):
        p = p_ref[:, lo:lo + width]
        above = pltpu.roll(p, 1, axis=0)
        above = jnp.where(rows == 0, prev_ref[SUBLANE - 1:SUBLANE, lo:lo + width], above)
        above = jnp.where(seq_start, 0.0, above)
        return p + mu_ref[:, lo:lo + width] * (above - p)

    o_r, o_k, o_v, o_w, w_w, o_a, w_a, o_g, w_g, o_xv, w_xv = seg
    r = shifted_lerp(o_r, rw)
    kr = shifted_lerp(o_k, rw)
    vr = shifted_lerp(o_v, rw)
    xw = shifted_lerp(o_w, w_w)
    xa = shifted_lerp(o_a, w_a)
    xg = shifted_lerp(o_g, w_g)

    w_log = -_softplus(-(w0_ref[...] + _dot(jnp.tanh(xw).astype(BF16), w2_ref[...]))) - 0.5
    decay = jnp.exp(-jnp.exp(w_log))
    a = _sigmoid(a0_ref[...] + _dot(xa.astype(BF16), a2_ref[...]))
    g = _dot(_sigmoid(xg).astype(BF16), g2_ref[...])
    if has_vfirst:
        xv = shifted_lerp(o_xv, w_xv)
        vr = vr + (vf_ref[...] - vr) * _sigmoid(v0_ref[...] + _dot(xv.astype(BF16), v2_ref[...]))
    kk = kr * kk_ref[...]
    kk = kk * lax.rsqrt(jnp.maximum(_head_sums([kk * kk])[0], 1e-24))
    r_out[...] = r
    w_out[...] = decay
    k_out[...] = kr * (1.0 + (a - 1.0) * ka_ref[...])
    v_out[...] = vr
    a_out[...] = -kk
    b_out[...] = kk * a
    g_out[...] = g


def rw_prep(p_rw, seg, rw, mu, w2, w0, a2, a0, g2, kk, ka, v2, v0, v_first, n_prompt, t_prompt):
    m, n_rw = p_rw.shape
    tr = _tile(m, 128, SUBLANE)
    nsub = tr // SUBLANE
    has_vfirst = v_first is not None
    row = lambda w: pl.BlockSpec((1, w), lambda i: (0, 0))
    full = lambda x: pl.BlockSpec(x.shape, lambda i: (0, 0))
    tile = pl.BlockSpec((tr, rw), lambda i: (i, 0))
    in_specs = [pl.BlockSpec((tr, n_rw), lambda i: (i, 0)),
                pl.BlockSpec((SUBLANE, n_rw), lambda i: (jnp.maximum(i * nsub - 1, 0), 0))]
    args = [p_rw, p_rw]
    if has_vfirst:
        in_specs.append(tile)
        args.append(v_first)
    in_specs += [row(n_rw), full(w2), row(rw), full(a2), row(rw), full(g2), row(rw), row(rw)]
    args += [mu.reshape(1, n_rw), w2, w0.reshape(1, rw), a2, a0.reshape(1, rw), g2,
             kk.reshape(1, rw), ka.reshape(1, rw)]
    if has_vfirst:
        in_specs += [full(v2), row(rw)]
        args += [v2, v0.reshape(1, rw)]
    blocks = [_nbytes((tr + SUBLANE, n_rw), F32), (7 + has_vfirst) * _nbytes((tr, rw), F32),
              sum(_nbytes(x.shape, x.dtype) for x in (w2, a2, g2)) + (_nbytes(v2.shape, v2.dtype) if has_vfirst else 0)]
    outs = pl.pallas_call(
        functools.partial(_rw_prep_kernel, rw=rw, seg=seg, has_vfirst=has_vfirst,
                          n_prompt=n_prompt, t_prompt=t_prompt),
        grid=(m // tr,),
        in_specs=in_specs,
        out_specs=[tile] * 7,
        out_shape=[jax.ShapeDtypeStruct((m, rw), F32)] * 7,
        compiler_params=_params(("arbitrary",), blocks, 12 * _nbytes((tr, rw), F32)),
        name="rwkv_prep",
    )(*args)
    return outs


def _rw_scan_kernel(*refs, nb, n_groups, tc, t_start, split_in, layer, owns_stack):
    n_op = 7 * (nb if split_in else 1)
    op_refs = refs[:n_op]
    gnw_ref, gnb_ref, rk_ref, s0_ref = refs[n_op:n_op + 4]
    o_ref, sout_ref, s_scr, lhs_scr, y8_scr, y_scr = refs[-6:]
    c = pl.program_id(1)
    units = [(s, grp) for s in range(nb) for grp in range(n_groups)]

    def operand(kind, s):
        return (op_refs[kind * nb + s], 0) if split_in else (op_refs[kind], s * tc)

    @pl.when(c == 0)
    def _():
        for u, (s, grp) in enumerate(units):
            s_scr[u * RW_HEAD:(u + 1) * RW_HEAD, :] = jnp.concatenate(
                [s0_ref[s, UNIT_HEADS * grp + h] for h in range(UNIT_HEADS)], axis=1)

    bd = _block_diag_ones(UNIT_LANES, RW_HEAD, BF16)
    diag = (lax.broadcasted_iota(jnp.int32, (RW_HEAD, UNIT_LANES), 1) % RW_HEAD
            == lax.broadcasted_iota(jnp.int32, (RW_HEAD, UNIT_LANES), 0))

    def group_step(base, first):
        def row(kind, u, j):
            s, grp = units[u]
            ref, off = operand(kind, s)
            return ref[pl.ds(off + base, SUBLANE), grp * UNIT_LANES:(grp + 1) * UNIT_LANES][j:j + 1, :]

        n_u = len(units)
        q0 = 2 * n_u * RW_HEAD

        def put_sa_v(par, u, s, j):
            lhs_scr[par, 2 * u * RW_HEAD:(2 * u + 1) * RW_HEAD, :] = (s * row(4, u, j)).astype(BF16)
            lhs_scr[par, (2 * u + 1) * RW_HEAD:(2 * u + 2) * RW_HEAD, :] = (
                jnp.where(diag, row(3, u, j), 0.0).astype(BF16))

        if first:
            y8_scr[...] = jnp.zeros(y8_scr.shape, F32)
        for u in range(n_u):
            put_sa_v(0, u, s_scr[u * RW_HEAD:(u + 1) * RW_HEAD, :], first)
        bc = _dot(lhs_scr[0, :q0], bd)
        for j in range(first, SUBLANE):
            par = (j - first + 1) % 2
            last = j == SUBLANE - 1
            for u in range(n_u):
                rows = slice(u * RW_HEAD, (u + 1) * RW_HEAD)
                s = (s_scr[rows, :] * row(1, u, j)
                     + bc[2 * u * RW_HEAD:(2 * u + 1) * RW_HEAD] * row(5, u, j)
                     + bc[(2 * u + 1) * RW_HEAD:(2 * u + 2) * RW_HEAD] * row(2, u, j))
                s_scr[rows, :] = s
                lhs_scr[par, q0 + u * RW_HEAD:q0 + (u + 1) * RW_HEAD, :] = (s * row(0, u, j)).astype(BF16)
                if not last:
                    put_sa_v(par, u, s, j + 1)
            bc = _dot(lhs_scr[par, q0:] if last else lhs_scr[par], bd)
            y_bc = bc if last else bc[q0:]
            for u in range(n_u):
                y_row = jnp.sum(jnp.where(diag, y_bc[u * RW_HEAD:(u + 1) * RW_HEAD], 0.0), axis=0, keepdims=True)
                y8_scr[u * SUBLANE + j:u * SUBLANE + j + 1, :] = y_row
        for u, (s, grp) in enumerate(units):
            y_scr[pl.ds(s * tc + base, SUBLANE), grp * UNIT_LANES:(grp + 1) * UNIT_LANES] = (
                y8_scr[u * SUBLANE:(u + 1) * SUBLANE, :])

    if tc == SUBLANE:
        group_step(0, t_start)
    else:
        assert t_start == 0 and tc % SUBLANE == 0

        def body(i, carry):
            group_step(pl.multiple_of(i * SUBLANE, SUBLANE), 0)
            return carry

        lax.fori_loop(0, tc // SUBLANE, body, 0)

    for s in range(nb):
        def full(kind):
            ref, off = operand(kind, s)
            return ref[off:off + tc, :]
        y = y_scr[s * tc:(s + 1) * tc, :]
        mean, rk_sum = _head_sums([y, full(0) * full(2) * rk_ref[...]])
        yc = y - mean * (1.0 / RW_HEAD)
        var = _head_sums([yc * yc])[0] * (1.0 / RW_HEAD)
        yn = yc * lax.rsqrt(var + GN_EPS) * gnw_ref[...] + gnb_ref[...]
        o_ref[s] = ((yn + rk_sum * full(3)) * full(6)).astype(o_ref.dtype)

    @pl.when(c == pl.num_programs(1) - 1)
    def _():
        out = _state_out(sout_ref, layer, owns_stack)
        for u, (s, grp) in enumerate(units):
            st = s_scr[u * RW_HEAD:(u + 1) * RW_HEAD, :]
            for h in range(UNIT_HEADS):
                out[s, UNIT_HEADS * grp + h] = st[:, h * RW_HEAD:(h + 1) * RW_HEAD]


def _state_io(s0, s0_layer, layer, depth, s_prev, nb, n_inputs):
    blk = (None, nb) + s0.shape[2:]
    zeros = (0,) * (s0.ndim - 2)
    in_spec = pl.BlockSpec(blk, lambda i, c: (s0_layer, i) + zeros)
    out_shape = jax.ShapeDtypeStruct((depth,) + s0.shape[1:], s0.dtype)
    if s_prev is None:
        out_spec = pl.BlockSpec((depth, nb) + s0.shape[2:], lambda i, c: (0, i) + zeros)
        return in_spec, out_spec, out_shape, [], [], {}
    out_spec = pl.BlockSpec(blk, lambda i, c: (layer, i) + zeros)
    return in_spec, out_spec, out_shape, [pl.BlockSpec(memory_space=pl.ANY)], [s_prev], {n_inputs: 1}


def _state_out(sout_ref, layer, owns_stack):
    if not owns_stack:
        return sout_ref
    for other in range(sout_ref.shape[0]):
        if other != layer:
            sout_ref[other] = jnp.zeros(sout_ref.shape[1:], sout_ref.dtype)
    return sout_ref.at[layer]


def rw_scan(ops, g, gn_w, gn_b, rk, s0, s0_layer, layer, depth, s_prev, row0, n_seq, t_seq, tc, t_start, nb):
    rw = ops[0].shape[1]
    n_heads = s0.shape[2]
    assert n_heads % UNIT_HEADS == 0
    n_groups = n_heads // UNIT_HEADS
    n_chunks = t_seq // tc
    split_in = n_chunks > 1
    assert n_seq % nb == 0 and (split_in or (row0 % (nb * tc) == 0 and tc == t_seq))
    if split_in:
        base = row0 // tc
        in_specs = [pl.BlockSpec((tc, rw), lambda i, c, s=s: (base + (i * nb + s) * n_chunks + c, 0))
                    for _ in range(7) for s in range(nb)]
        args = [x for x in (*ops, g) for _ in range(nb)]
    else:
        base = row0 // (nb * tc)
        in_specs = [pl.BlockSpec((nb * tc, rw), lambda i, c: (base + i, 0))] * 7
        args = [*ops, g]
    row = pl.BlockSpec((1, rw), lambda i, c: (0, 0))
    st_in, st_out, st_shape, prev_spec, prev_arg, alias = _state_io(
        s0, s0_layer, layer, depth, s_prev, nb, len(args) + 4)
    n_units = nb * n_groups
    scratch = [pltpu.VMEM((n_units * RW_HEAD, UNIT_LANES), F32),
               pltpu.VMEM((2, 3 * n_units * RW_HEAD, UNIT_LANES), BF16),
               pltpu.VMEM((n_units * SUBLANE, UNIT_LANES), F32),
               pltpu.VMEM((nb * tc, rw), F32)]
    blocks = [7 * _nbytes((nb * tc, rw), F32), _nbytes((nb * tc, rw), BF16),
              (1 + depth) * _nbytes((nb * n_heads, RW_HEAD, LANE), F32)]
    o, s_out = pl.pallas_call(
        functools.partial(_rw_scan_kernel, nb=nb, n_groups=n_groups, tc=tc, t_start=t_start, split_in=split_in,
                          layer=layer, owns_stack=s_prev is None),
        grid=(n_seq // nb, n_chunks),
        in_specs=in_specs + [row] * 3 + [st_in] + prev_spec,
        out_specs=[pl.BlockSpec((nb, tc, rw), lambda i, c: (i, c, 0)), st_out],
        out_shape=[jax.ShapeDtypeStruct((n_seq, t_seq, rw), BF16), st_shape],
        input_output_aliases=alias,
        scratch_shapes=scratch,
        compiler_params=_params(("arbitrary", "arbitrary"), blocks,
                                16 * _nbytes((n_units * RW_HEAD, UNIT_LANES), F32) + 8 * _nbytes((tc, rw), F32)),
        name="rwkv_scan",
    )(*args, gn_w.reshape(1, rw), gn_b.reshape(1, rw), rk.reshape(1, rw), s0, *prev_arg)
    return o.reshape(n_seq * t_seq, rw), s_out


def _gla_kernel(*refs, n_heads, dk, dv, t_start, layer, owns_stack):
    q_ref, k_ref, v_ref, go_ref, lr_ref, w2_ref, gb_ref, gn_ref, s0_ref = refs[:9]
    o_ref, sout_ref, s_scr = refs[-3:]
    c_rows = q_ref.shape[0]
    n = pl.program_id(1)

    @pl.when(n == 0)
    def _():
        s_scr[...] = s0_ref[0]

    rows = lax.broadcasted_iota(jnp.int32, (c_rows, 1), 0)
    valid = rows >= t_start
    tri = (lax.broadcasted_iota(jnp.int32, (c_rows, c_rows), 0)
           >= lax.broadcasted_iota(jnp.int32, (c_rows, c_rows), 1))
    tri_f = jnp.where(tri, 1.0, 0.0).astype(F32)
    lr = lr_ref[...]
    mid = max(c_rows // 2 - 1, 0)
    scale = dk ** -0.5
    for h in range(n_heads):
        ks = slice(h * dk, (h + 1) * dk)
        vs = slice(h * dv, (h + 1) * dv)
        x = _dot(lr, w2_ref[:, ks], precision=HIGHEST) + gb_ref[:, ks]
        log_a = (jnp.minimum(x, 0.0) - jnp.log(1.0 + jnp.exp(-jnp.abs(x)))) * (1.0 / GLA_GATE_TAU)
        log_a = jnp.where(valid, log_a, 0.0)
        b = _dot(tri_f, log_a, precision=HIGHEST)
        b_last = b[c_rows - 1:c_rows, :]
        b_mid = b[mid:mid + 1, :]
        q = q_ref[:, ks] * scale
        k = jnp.where(valid, k_ref[:, ks], 0.0)
        v = v_ref[:, vs].astype(BF16)
        qi = (q * jnp.exp(b - b_mid)).astype(BF16)
        ki = (k * jnp.exp(b_mid - b)).astype(BF16)
        att = lax.dot_general(qi, ki, (((1,), (1,)), ((), ())), preferred_element_type=F32)
        att = jnp.where(tri, att, 0.0).astype(BF16)
        s = s_scr[h]
        o = _dot(att, v) + _dot((q * jnp.exp(b)).astype(BF16), s.astype(BF16))
        kd = (k * jnp.exp(b_last - b)).astype(BF16)
        upd = lax.dot_general(kd, v, (((0,), (0,)), ((), ())), preferred_element_type=F32)
        dcol = jnp.transpose(jnp.broadcast_to(jnp.exp(b_last), (LANE, dk)))
        s_scr[h] = s * jnp.concatenate([dcol] * (dv // LANE), axis=1) + upd
        on = o * lax.rsqrt(jnp.mean(o * o, axis=-1, keepdims=True) + HEAD_NORM_EPS)
        go = go_ref[:, vs]
        o_ref[:, vs] = (on * gn_ref[:, vs] * (go * _sigmoid(go))).astype(o_ref.dtype)

    @pl.when(n == pl.num_programs(1) - 1)
    def _():
        _state_out(sout_ref, layer, owns_stack)[0] = s_scr[...]


def gla(p_gla, p_rw, lr_block, w2p, gb, gn, s0, s0_layer, layer, depth, s_prev, row0, n_seq, t_seq, c_rows, t_start):
    _, _, n_heads, dk, dv = s0.shape
    qk, gw = n_heads * dk, n_heads * dv
    n_chunks = t_seq // c_rows
    base = row0 // c_rows
    rmap = lambda col: (lambda i, n: (base + i * n_chunks + n, col))
    st_in, st_out, st_shape, prev_spec, prev_arg, alias = _state_io(s0, s0_layer, layer, depth, s_prev, 1, 9)
    const = lambda shape: pl.BlockSpec(shape, lambda i, n: (0, 0))
    blocks = [_nbytes((c_rows, 2 * qk + 2 * gw + LANE), F32), _nbytes((LANE + 2, qk), F32),
              _nbytes((c_rows, gw), BF16), (1 + depth) * _nbytes((n_heads, dk, dv), F32)]
    o, s_out = pl.pallas_call(
        functools.partial(_gla_kernel, n_heads=n_heads, dk=dk, dv=dv, t_start=t_start,
                          layer=layer, owns_stack=s_prev is None),
        grid=(n_seq, n_chunks),
        in_specs=[pl.BlockSpec((c_rows, qk), rmap(0)),
                  pl.BlockSpec((c_rows, qk), rmap(1)),
                  pl.BlockSpec((c_rows, gw), rmap(2 * qk // gw)),
                  pl.BlockSpec((c_rows, gw), rmap(2 * qk // gw + 1)),
                  pl.BlockSpec((c_rows, LANE), rmap(lr_block)),
                  const((LANE, qk)), const((1, qk)), const((1, gw)), st_in] + prev_spec,
        out_specs=[pl.BlockSpec((c_rows, gw), lambda i, n: (i * n_chunks + n, 0)), st_out],
        out_shape=[jax.ShapeDtypeStruct((n_seq * t_seq, gw), BF16), st_shape],
        input_output_aliases=alias,
        scratch_shapes=[pltpu.VMEM((n_heads, dk, dv), F32)],
        compiler_params=_params(("arbitrary", "arbitrary"), blocks,
                                _nbytes((n_heads, dk, dv), F32) + 6 * _nbytes((dk, dv), F32)),
        name="gla",
    )(p_gla, p_gla, p_gla, p_gla, p_rw, w2p, gb.reshape(1, qk), gn.reshape(1, gw), s0, *prev_arg)
    return o, s_out


def _router_kernel(x_ref, g_ref, wr_ref, h_ref, info_ref, *, n_experts):
    x = x_ref[...]
    h = x * lax.rsqrt(jnp.mean(x * x, axis=-1, keepdims=True) + NORM_EPS) * g_ref[...]
    half = h.shape[1] // 2
    lo = lax.bitcast_convert_type(h[:, :half].astype(BF16).astype(F32), jnp.uint32)
    hi = lax.bitcast_convert_type(h[:, half:].astype(BF16).astype(F32), jnp.uint32)
    h_ref[...] = (lo >> 16) | (hi & jnp.uint32(0xFFFF0000))
    logits = _dot(h, wr_ref[...], precision=HIGHEST)
    lane = lax.broadcasted_iota(jnp.int32, logits.shape, 1)
    neg = jnp.float32(-jnp.inf)
    logits = jnp.where(lane < n_experts, logits, neg)
    m1 = jnp.max(logits, axis=-1, keepdims=True)
    i1 = jnp.min(jnp.where(logits == m1, lane, LANE), axis=-1, keepdims=True)
    rest = jnp.where(lane == i1, neg, logits)
    m2 = jnp.max(rest, axis=-1, keepdims=True)
    i2 = jnp.min(jnp.where(rest == m2, lane, LANE), axis=-1, keepdims=True)
    e2 = jnp.exp(m2 - m1)
    g1 = 1.0 / (1.0 + e2)
    g2 = e2 / (1.0 + e2)
    info = jnp.where(lane == 0, i1.astype(F32),
                     jnp.where(lane == 1, i2.astype(F32),
                               jnp.where(lane == 2, g1, jnp.where(lane == 3, g2, 0.0))))
    info_ref[...] = info


def moe_router(x, g, router):
    m, d = x.shape
    n_experts = router.shape[1]
    tr = _tile(m, 256, SUBLANE)
    wr = jnp.zeros((d, LANE), F32).at[:, :n_experts].set(router)
    blocks = [_nbytes((tr, d), F32), _nbytes((d, LANE), F32), _nbytes((tr, d // 2), jnp.uint32)]
    return pl.pallas_call(
        functools.partial(_router_kernel, n_experts=n_experts),
        grid=(m // tr,),
        in_specs=[pl.BlockSpec((tr, d), lambda i: (i, 0)), pl.BlockSpec((1, d), lambda i: (0, 0)),
                  pl.BlockSpec((d, LANE), lambda i: (0, 0))],
        out_specs=[pl.BlockSpec((tr, d // 2), lambda i: (i, 0)), pl.BlockSpec((tr, LANE), lambda i: (i, 0))],
        out_shape=[jax.ShapeDtypeStruct((m, d // 2), jnp.uint32), jax.ShapeDtypeStruct((m, LANE), F32)],
        compiler_params=_params(("arbitrary",), blocks, 4 * _nbytes((tr, d), F32)),
        name="moe_router",
    )(x, g.reshape(1, d), wr)


def _gather_kernel(idx_ref, src_ref, out_ref, buf, sem):
    tg, w = buf.shape

    def copy(r):
        return pltpu.make_async_copy(src_ref.at[pl.ds(idx_ref[0, 0, r], 1)], buf.at[pl.ds(r, 1)], sem)

    def start(r, carry):
        copy(r).start()
        return carry

    def wait(r, carry):
        copy(r).wait()
        return carry

    lax.fori_loop(0, tg, start, 0)
    lax.fori_loop(0, tg, wait, 0)
    words = buf[...]
    out_ref[:, :w] = lax.bitcast_convert_type(words << 16, F32).astype(BF16)
    out_ref[:, w:] = lax.bitcast_convert_type(words & jnp.uint32(0xFFFF0000), F32).astype(BF16)


def gather_rows(src, idx, tg=256):
    n = idx.shape[0]
    tg = _tile(n, tg, 16)
    w = src.shape[1]
    return pl.pallas_call(
        _gather_kernel,
        grid=(n // tg,),
        in_specs=[pl.BlockSpec((1, 1, tg), lambda i: (i, 0, 0), memory_space=pltpu.SMEM),
                  pl.BlockSpec(memory_space=pl.ANY)],
        out_specs=pl.BlockSpec((tg, 2 * w), lambda i: (i, 0)),
        out_shape=jax.ShapeDtypeStruct((n, 2 * w), BF16),
        scratch_shapes=[pltpu.VMEM((tg, w), src.dtype), pltpu.SemaphoreType.DMA(())],
        compiler_params=_params(("arbitrary",), [_nbytes((tg, 2 * w), BF16)], 4 * _nbytes((tg, w), src.dtype)),
        name="moe_gather",
    )(idx.reshape(n // tg, 1, tg), src)


def _moe_gateup_kernel(te_ref, tf_ref, tv_ref, a_ref, wg_ref, wu_ref, o_ref, wgb_ref, wub_ref):
    t = pl.program_id(1)

    @pl.when(tf_ref[t] == 1)
    def _():
        wgb_ref[...] = wg_ref[...].astype(BF16)
        wub_ref[...] = wu_ref[...].astype(BF16)

    @pl.when(tv_ref[t] == 1)
    def _():
        a = a_ref[...]
        g = _dot(a, wgb_ref[...])
        u = _dot(a, wub_ref[...])
        o_ref[...] = (g * _sigmoid(g) * u).astype(o_ref.dtype)

    @pl.when(tv_ref[t] == 0)
    def _():
        o_ref[...] = jnp.zeros(o_ref.shape, o_ref.dtype)


def moe_gateup(xs, wg, wu, tile_expert, tile_first, tile_valid, tm, tn_pref=512):
    r, d = xs.shape
    f = wg.shape[2]
    tn = _tile(f, tn_pref, LANE)
    wspec = pl.BlockSpec((None, d, tn), lambda j, t, te, tf, tv: (te[t], 0, j))
    blocks = [_nbytes((tm, d), BF16), 2 * _nbytes((d, tn), F32), _nbytes((tm, tn), BF16)]
    return pl.pallas_call(
        _moe_gateup_kernel,
        grid_spec=pltpu.PrefetchScalarGridSpec(
            num_scalar_prefetch=3,
            grid=(f // tn, r // tm),
            in_specs=[pl.BlockSpec((tm, d), lambda j, t, te, tf, tv: (t, 0)), wspec, wspec],
            out_specs=pl.BlockSpec((tm, tn), lambda j, t, te, tf, tv: (t, j)),
            scratch_shapes=[pltpu.VMEM((d, tn), BF16), pltpu.VMEM((d, tn), BF16)]),
        out_shape=jax.ShapeDtypeStruct((r, f), BF16),
        compiler_params=_params(("arbitrary", "arbitrary"), blocks,
                                2 * _nbytes((d, tn), BF16) + 3 * _nbytes((tm, tn), F32)),
        name="moe_gateup",
    )(tile_expert, tile_first, tile_valid, xs, wg, wu)


def _moe_down_kernel(te_ref, tf_ref, tv_ref, a_ref, w_hbm, o_ref, wb_ref, stage, sem, *, nk, tk, tn):
    j = pl.program_id(0)
    t = pl.program_id(1)

    def chunk_copy(c):
        src = w_hbm.at[te_ref[t], pl.ds(c * tk, tk), pl.ds(pl.multiple_of(j * tn, LANE), tn)]
        return pltpu.make_async_copy(src, stage.at[c % 2], sem.at[c % 2])

    def chunk_dot(c):
        return _dot(a_ref[:, c * tk:(c + 1) * tk], wb_ref[c])

    @pl.when(tf_ref[t] == 1)
    def _():
        chunk_copy(0).start()
        acc = None
        for c in range(nk):
            if c + 1 < nk:
                chunk_copy(c + 1).start()
            chunk_copy(c).wait()
            wb_ref[c] = stage[c % 2].astype(BF16)
            acc = chunk_dot(c) if acc is None else acc + chunk_dot(c)
        o_ref[...] = acc

    @pl.when(jnp.logical_and(tf_ref[t] == 0, tv_ref[t] == 1))
    def _():
        acc = chunk_dot(0)
        for c in range(1, nk):
            acc = acc + chunk_dot(c)
        o_ref[...] = acc

    @pl.when(tv_ref[t] == 0)
    def _():
        o_ref[...] = jnp.zeros(o_ref.shape, o_ref.dtype)


def moe_down(act, wd, tile_expert, tile_first, tile_valid, tm, tn_pref=512, tk_pref=1792):
    r, f = act.shape
    d = wd.shape[2]
    tn = _tile(d, tn_pref, LANE)
    tk = _tile(f, tk_pref, LANE)
    nk = f // tk
    blocks = [_nbytes((tm, f), BF16), _nbytes((tm, tn), F32)]
    return pl.pallas_call(
        functools.partial(_moe_down_kernel, nk=nk, tk=tk, tn=tn),
        grid_spec=pltpu.PrefetchScalarGridSpec(
            num_scalar_prefetch=3,
            grid=(d // tn, r // tm),
            in_specs=[pl.BlockSpec((tm, f), lambda j, t, te, tf, tv: (t, 0)),
                      pl.BlockSpec(memory_space=pl.ANY)],
            out_specs=pl.BlockSpec((tm, tn), lambda j, t, te, tf, tv: (t, j)),
            scratch_shapes=[pltpu.VMEM((nk, tk, tn), BF16), pltpu.VMEM((2, tk, tn), F32),
                            pltpu.SemaphoreType.DMA((2,))]),
        out_shape=jax.ShapeDtypeStruct((r, d), F32),
        compiler_params=_params(("arbitrary", "arbitrary"), blocks,
                                _nbytes((f, tn), BF16) + 2 * _nbytes((tk, tn), F32) + 2 * _nbytes((tm, tn), F32)),
        name="moe_down",
    )(tile_expert, tile_first, tile_valid, act, wd)


def _combine_kernel(p1_ref, p2_ref, x_ref, info_ref, ys_ref, o_ref, buf1, buf2, sem):
    tc = x_ref.shape[0]

    def copies(r):
        return (pltpu.make_async_copy(ys_ref.at[pl.ds(p1_ref[0, 0, r], 1)], buf1.at[pl.ds(r, 1)], sem.at[0]),
                pltpu.make_async_copy(ys_ref.at[pl.ds(p2_ref[0, 0, r], 1)], buf2.at[pl.ds(r, 1)], sem.at[1]))

    def start(r, carry):
        c1, c2 = copies(r)
        c1.start()
        c2.start()
        return carry

    def wait(r, carry):
        c1, c2 = copies(r)
        c1.wait()
        c2.wait()
        return carry

    lax.fori_loop(0, tc, start, 0)
    lax.fori_loop(0, tc, wait, 0)
    info = info_ref[...]
    o_ref[...] = x_ref[...] + info[:, 2:3] * buf1[...] + info[:, 3:4] * buf2[...]


def moe_combine(x, info, ys, pos1, pos2, tc=128):
    m, d = x.shape
    tc = _tile(m, tc, SUBLANE)
    idx = pl.BlockSpec((1, 1, tc), lambda i: (i, 0, 0), memory_space=pltpu.SMEM)
    blocks = [2 * _nbytes((tc, d), F32), _nbytes((tc, LANE), F32)]
    return pl.pallas_call(
        _combine_kernel,
        grid=(m // tc,),
        in_specs=[idx, idx, pl.BlockSpec((tc, d), lambda i: (i, 0)), pl.BlockSpec((tc, LANE), lambda i: (i, 0)),
                  pl.BlockSpec(memory_space=pl.ANY)],
        out_specs=pl.BlockSpec((tc, d), lambda i: (i, 0)),
        out_shape=jax.ShapeDtypeStruct((m, d), F32),
        scratch_shapes=[pltpu.VMEM((tc, d), F32), pltpu.VMEM((tc, d), F32), pltpu.SemaphoreType.DMA((2,))],
        compiler_params=_params(("arbitrary",), blocks, 4 * _nbytes((tc, d), F32)),
        name="moe_combine",
    )(pos1.reshape(m // tc, 1, tc), pos2.reshape(m // tc, 1, tc), x, info, ys)


def moe_ffn(x, norm_g, router, wg, wu, wd, row_valid, n_valid, tm=384):
    m, d = x.shape
    n_experts = router.shape[1]
    h_packed, info = moe_router(x, norm_g, router)

    none = n_experts
    e1 = jnp.where(row_valid, info[:, 0].astype(jnp.int32), none)
    e2 = jnp.where(row_valid, info[:, 1].astype(jnp.int32), none)
    e_all = jnp.concatenate([e1, e2])
    onehot = (e_all[:, None] == jnp.arange(n_experts, dtype=jnp.int32)[None, :]).astype(jnp.int32)
    counts = jnp.sum(onehot, axis=0)
    rank = jnp.sum((jnp.cumsum(onehot, axis=0) - onehot) * onehot, axis=1)
    padded = (counts + tm - 1) // tm * tm
    ends = jnp.cumsum(padded)
    starts = ends - padded
    n_rows = _round_up(2 * n_valid + n_experts * (tm - 1), tm)
    n_tiles = n_rows // tm
    pos = jnp.where(e_all < none, jnp.take(starts, jnp.minimum(e_all, none - 1)) + rank, n_rows)
    tok = jnp.tile(jnp.arange(m, dtype=jnp.int32), 2)
    row_src = jnp.zeros((n_rows,), jnp.int32).at[pos].set(tok, mode="drop")
    tile_start = jnp.arange(n_tiles, dtype=jnp.int32) * tm
    tile_valid = (tile_start < ends[-1]).astype(jnp.int32)
    last_tile = jnp.maximum(ends[-1] // tm - 1, 0)
    tile_expert = jnp.searchsorted(ends, jnp.minimum(tile_start, last_tile * tm), side="right").astype(jnp.int32)
    tile_expert = jnp.minimum(tile_expert, n_experts - 1)
    prev_expert = jnp.concatenate([jnp.full((1,), -1, jnp.int32), tile_expert[:-1]])
    tile_first = jnp.logical_and(tile_expert != prev_expert, tile_valid == 1).astype(jnp.int32)
    tile_first = tile_first.at[0].set(1)
    pos_safe = jnp.where(pos < n_rows, pos, 0)
    gates = jnp.where(row_valid[:, None], info, 0.0)

    xs = gather_rows(h_packed, row_src)
    act = moe_gateup(xs, wg, wu, tile_expert, tile_first, tile_valid, tm)
    ys = moe_down(act, wd, tile_expert, tile_first, tile_valid, tm)
    return moe_combine(x, gates, ys, pos_safe[:m], pos_safe[m:])


def _pad_cols(w, width):
    return w if w.shape[-1] == width else jnp.pad(w, [(0, 0)] * (w.ndim - 1) + [(0, width - w.shape[-1])])


def _pad_rows(w, rows):
    return w if w.shape[0] == rows else jnp.pad(w, [(0, rows - w.shape[0])] + [(0, 0)] * (w.ndim - 1))


def kernel(x_prompt, x_sample, state_gla, state_rwkv, state_shift, norm_mix, norm_ffn, norm_final,
           w_in_first, w_in_rest, mu_first, mu_rest, gla_w2, gla_b, gla_norm,
           rw_w2, rw_w0, rw_a2, rw_a0, rw_g2, rw_kk, rw_ka, rw_rk, rw_gn_w, rw_gn_b, rw_v2, rw_v0,
           w_out, ffn_w_gate, ffn_w_up, ffn_w_down, moe_router_w, moe_w_gate, moe_w_up, moe_w_down):
    batch, seq, d = x_prompt.shape
    dec_batch, dec_seq, _ = x_sample.shape
    depth = norm_mix.shape[0]
    _, _, gla_heads, dk, dv = state_gla.shape
    rw_heads = state_rwkv.shape[2]
    assert state_rwkv.shape[3] == RW_HEAD and state_rwkv.shape[4] == RW_HEAD and rw_heads % 2 == 0
    qk, gw, rw = gla_heads * dk, gla_heads * dv, rw_heads * RW_HEAD
    rank = gla_w2.shape[1]
    p_gla = 2 * qk + 2 * gw + rank
    assert dk % LANE == 0 and dv % LANE == 0 and (2 * qk) % gw == 0 and rank <= LANE

    group = _round_up(dec_seq + 1, SUBLANE)
    t_start = group - dec_seq
    n_prompt = batch * seq
    m = n_prompt + dec_batch * group
    prompt_chunk = _tile(seq, 64, SUBLANE)
    scan_chunk = _tile(seq, 64, SUBLANE)
    sample_rows = jnp.arange(dec_batch * group, dtype=jnp.int32) % group >= t_start
    row_valid = jnp.concatenate([jnp.ones((n_prompt,), bool), sample_rows])

    x = jnp.concatenate([
        x_prompt.reshape(n_prompt, d),
        jnp.concatenate([jnp.zeros((dec_batch, t_start, d), F32), x_sample], axis=1).reshape(dec_batch * group, d)])

    lw, la, lg = rw_w2.shape[1], rw_a2.shape[1], rw_g2.shape[1]
    lv = rw_v2.shape[1]
    pw, pa, pg, pv = (_round_up(n, LANE) for n in (lw, la, lg, lv))
    o_w = 3 * rw
    o_a = o_w + pw
    o_g = o_a + pa
    o_xv = o_g + pg
    o_lr = o_xv + pv
    n_rw = _round_up(o_lr + LANE, 512)
    seg = (0, rw, 2 * rw, o_w, pw, o_a, pa, o_g, pg, o_xv, pv)

    def repack(wt, mu_l, has_v):
        wr = wt[p_gla:]
        parts_w, parts_mu = [], []
        src = 0
        for width, padded in ((3 * rw, 3 * rw), (lw, pw), (la, pa), (lg, pg)) + (((lv, pv),) if has_v else ()):
            parts_w.append(_pad_rows(wr[src:src + width], padded))
            parts_mu.append(_pad_cols(mu_l[src:src + width], padded))
            src += width
        if not has_v:
            parts_w.append(jnp.zeros((pv, d), F32))
            parts_mu.append(jnp.zeros((pv,), F32))
        parts_w.append(_pad_rows(wt[p_gla - rank:p_gla], n_rw - o_lr))
        parts_mu.append(jnp.zeros((n_rw - o_lr,), F32))
        return jnp.concatenate(parts_w, axis=0).astype(BF16), jnp.concatenate(parts_mu)

    zeros_gla = jnp.zeros((1, batch, gla_heads, dk, dv), F32)
    zeros_rw = jnp.zeros((1, batch, rw_heads, RW_HEAD, RW_HEAD), F32)
    last_rows = jnp.concatenate([jnp.arange(batch, dtype=jnp.int32) * seq + seq - 1,
                                 n_prompt + jnp.arange(dec_batch, dtype=jnp.int32) * group + group - 1])
    n_last = _round_up(batch + dec_batch, SUBLANE)
    prev_rows = n_prompt + jnp.arange(dec_batch, dtype=jnp.int32) * group + t_start - 1

    new_gla_p = new_gla_s = new_rw_p = new_rw_s = None
    new_shift = []
    v_first = None
    for l in range(depth):
        h = rmsnorm(x, norm_mix[l], BF16).at[prev_rows].set(state_shift[l].astype(BF16))
        x_last = _pad_rows(jnp.take(x, last_rows, axis=0), n_last)
        new_shift.append(rmsnorm(x_last, norm_mix[l], F32)[:batch + dec_batch])

        if l == 0:
            wt, mu_l = w_in_first.T, mu_first
        else:
            wt, mu_l = w_in_rest[l - 1].T, mu_rest[l - 1]
        w_rw, mu_p = repack(wt, mu_l, l > 0)
        pg_all = matmul([h], wt, 2 * qk + 2 * gw, w_t=True, name="w_in_gla")
        pr_all = matmul([h], w_rw, n_rw, w_t=True, name="w_in_rwkv")

        ops = rw_prep(pr_all, seg, rw, mu_p,
                      _pad_rows(rw_w2[l], pw).astype(BF16), rw_w0[l], _pad_rows(rw_a2[l], pa).astype(BF16), rw_a0[l],
                      _pad_rows(rw_g2[l], pg).astype(BF16), rw_kk[l], rw_ka[l],
                      _pad_rows(rw_v2[l - 1], pv).astype(BF16) if l > 0 else None,
                      rw_v0[l - 1] if l > 0 else None, v_first, n_prompt, seq)
        r_, w_, k_, v_, a_, b_, g_ = ops
        if l == 0:
            v_first = v_
        scan_args = ((r_, w_, k_, v_, a_, b_), g_, rw_gn_w[l], rw_gn_b[l], rw_rk[l].reshape(rw))
        o_rw_p, new_rw_p = rw_scan(*scan_args, zeros_rw, 0, l, depth, new_rw_p,
                                   0, batch, seq, scan_chunk, 0, _tile(batch, 4, 1))
        o_rw_s, new_rw_s = rw_scan(*scan_args, state_rwkv, l, l, depth, new_rw_s,
                                   n_prompt, dec_batch, group, group, t_start, _tile(dec_batch, 4, 1))

        w2p = _pad_rows(gla_w2[l], LANE)
        gla_args = (pg_all, pr_all, o_lr // LANE, w2p, gla_b[l], gla_norm[l])
        o_gla_p, new_gla_p = gla(*gla_args, zeros_gla, 0, l, depth, new_gla_p, 0, batch, seq, prompt_chunk, 0)
        o_gla_s, new_gla_s = gla(*gla_args, state_gla, l, l, depth, new_gla_s,
                                 n_prompt, dec_batch, group, group, t_start)

        o_gla = jnp.concatenate([o_gla_p, o_gla_s])
        o_rw = jnp.concatenate([o_rw_p, o_rw_s])
        x = matmul([o_gla, o_rw], w_out, d, res=x, layer=l, name="w_out")

        i = l // 2
        if l % 2 == 0:
            h2 = rmsnorm(x, norm_ffn[l], BF16)
            act = gateup(h2, ffn_w_gate[i], ffn_w_up[i])
            x = matmul([act], ffn_w_down[i].astype(BF16), d, res=x, tm_pref=512, name="ffn_down")
        else:
            x = moe_ffn(x, norm_ffn[l], moe_router_w[i], moe_w_gate[i], moe_w_up[i], moe_w_down[i],
                        row_valid, n_prompt + dec_batch * dec_seq)

    y_prompt = rmsnorm(x, norm_final, F32, 0, n_prompt).reshape(batch, seq, d)
    y_sample = rmsnorm(x, norm_final, F32, n_prompt, m - n_prompt).reshape(dec_batch, group, d)[:, t_start:]
    shift = jnp.stack(new_shift)
    return (y_prompt, y_sample, new_gla_p, new_gla_s, new_rw_p, new_rw_s, shift[:, :batch], shift[:, batch:])
```

```python
import functools
import math

import jax
import jax.numpy as jnp
from jax import lax
from jax.experimental import pallas as pl
from jax.experimental.pallas import tpu as pltpu

F32 = jnp.float32
BF16 = jnp.bfloat16
HIGHEST = lax.Precision.HIGHEST

NORM_EPS = 1e-6
HEAD_NORM_EPS = 1e-5
GN_EPS = 64e-5
GLA_GATE_TAU = 16.0

LANE = 128
SUBLANE = 8
RW_HEAD = 64
MXU_DIM = 256
UNIT_HEADS = MXU_DIM // RW_HEAD
UNIT_LANES = MXU_DIM
VMEM_CAP = 64 << 20
VMEM_BUDGET = VMEM_CAP - (6 << 20)


def _round_up(n, m):
    return (n + m - 1) // m * m


def _tile(n, pref, align):
    t = min(pref, n)
    t -= t % align
    while t >= align:
        if n % t == 0:
            return t
        t -= align
    return n


def _nbytes(shape, dtype):
    return math.prod(shape) * jnp.dtype(dtype).itemsize


def _params(sem, blocks, scratch=0):
    need = 2 * sum(blocks) + scratch
    limit = min(VMEM_BUDGET, max(need + need // 4, 16 << 20))
    return pltpu.CompilerParams(dimension_semantics=sem, vmem_limit_bytes=int(limit))


def _sigmoid(x):
    return 1.0 / (1.0 + jnp.exp(-x))


def _softplus(x):
    return jnp.maximum(x, 0.0) + jnp.log(1.0 + jnp.exp(-jnp.abs(x)))


def _dot(a, b, **kw):
    return jnp.dot(a, b, preferred_element_type=F32, **kw)


def _rmsnorm_kernel(x_ref, g_ref, o_ref):
    x = x_ref[...]
    y = x * lax.rsqrt(jnp.mean(x * x, axis=-1, keepdims=True) + NORM_EPS)
    o_ref[...] = (y * g_ref[...]).astype(o_ref.dtype)


def rmsnorm(x, g, out_dtype, row0=0, rows=None):
    d = x.shape[1]
    m = x.shape[0] if rows is None else rows
    tr = _tile(math.gcd(m, row0) if row0 else m, 256, SUBLANE)
    first = row0 // tr
    return pl.pallas_call(
        _rmsnorm_kernel,
        grid=(m // tr,),
        in_specs=[pl.BlockSpec((tr, d), lambda i: (first + i, 0)), pl.BlockSpec((1, d), lambda i: (0, 0))],
        out_specs=pl.BlockSpec((tr, d), lambda i: (i, 0)),
        out_shape=jax.ShapeDtypeStruct((m, d), out_dtype),
        compiler_params=_params(("arbitrary",), [_nbytes((tr, d), F32), _nbytes((tr, d), out_dtype)]),
        name="rmsnorm",
    )(x, g.reshape(1, d))


def _mm_kernel(*refs, n_a, has_res, cast_w, w_t):
    a_refs = refs[:n_a]
    w_ref = refs[n_a]
    res_ref = refs[n_a + 1] if has_res else None
    o_ref = refs[n_a + 1 + has_res]
    if cast_w:
        wb_ref = refs[n_a + 2 + has_res]

        @pl.when(pl.program_id(1) == 0)
        def _():
            wb_ref[...] = w_ref[...].astype(BF16)
        w_ref = wb_ref
    acc = None
    off = 0
    for a_ref in a_refs:
        ka = a_ref.shape[1]
        if w_t:
            part = lax.dot_general(a_ref[...], w_ref[:, off:off + ka], (((1,), (1,)), ((), ())),
                                   preferred_element_type=F32)
        else:
            part = _dot(a_ref[...], w_ref[off:off + ka, :])
        acc = part if acc is None else acc + part
        off += ka
    if has_res:
        acc = acc + res_ref[...]
    o_ref[...] = acc.astype(o_ref.dtype)


def matmul(a_list, w, n_out, res=None, out_dtype=F32, tm_pref=1024, tn_pref=512, layer=None, w_t=False,
           name="matmul"):
    m = a_list[0].shape[0]
    k = w.shape[-1] if w_t else w.shape[-2]
    assert sum(a.shape[1] for a in a_list) == k and (w.ndim == 2) == (layer is None) and not (w_t and layer)
    tm = _tile(m, tm_pref, 16)
    tn = _tile(n_out, tn_pref, LANE)
    cast_w = w.dtype != BF16
    in_specs = [pl.BlockSpec((tm, a.shape[1]), lambda j, i: (i, 0)) for a in a_list]
    if w_t:
        in_specs.append(pl.BlockSpec((tn, k), lambda j, i: (j, 0)))
    elif layer is None:
        in_specs.append(pl.BlockSpec((k, tn), lambda j, i: (0, j)))
    else:
        in_specs.append(pl.BlockSpec((None, k, tn), lambda j, i: (layer, 0, j)))
    args = list(a_list) + [w]
    blocks = [_nbytes((tm, k), BF16), _nbytes((k, tn), w.dtype), _nbytes((tm, tn), out_dtype)]
    if res is not None:
        in_specs.append(pl.BlockSpec((tm, tn), lambda j, i: (i, j)))
        args.append(res)
        blocks.append(_nbytes((tm, tn), F32))
    scratch = [pltpu.VMEM((tn, k) if w_t else (k, tn), BF16)] if cast_w else []
    return pl.pallas_call(
        functools.partial(_mm_kernel, n_a=len(a_list), has_res=res is not None, cast_w=cast_w, w_t=w_t),
        grid=(n_out // tn, m // tm),
        in_specs=in_specs,
        out_specs=pl.BlockSpec((tm, tn), lambda j, i: (i, j)),
        out_shape=jax.ShapeDtypeStruct((m, n_out), out_dtype),
        scratch_shapes=scratch,
        compiler_params=_params(("arbitrary", "arbitrary"), blocks,
                                _nbytes((k, tn), BF16) * cast_w + 2 * _nbytes((tm, tn), F32)),
        name=name,
    )(*args)


def _gateup_kernel(a_ref, wg_ref, wu_ref, o_ref, wgb_ref, wub_ref):
    @pl.when(pl.program_id(1) == 0)
    def _():
        wgb_ref[...] = wg_ref[...].astype(BF16)
        wub_ref[...] = wu_ref[...].astype(BF16)
    a = a_ref[...]
    g = _dot(a, wgb_ref[...])
    u = _dot(a, wub_ref[...])
    o_ref[...] = (g * _sigmoid(g) * u).astype(o_ref.dtype)


def gateup(a, wg, wu, tm_pref=1024, tn_pref=256):
    m, k = a.shape
    f = wg.shape[1]
    tm = _tile(m, tm_pref, 16)
    tn = _tile(f, tn_pref, LANE)
    blocks = [_nbytes((tm, k), BF16), 2 * _nbytes((k, tn), F32), _nbytes((tm, tn), BF16)]
    return pl.pallas_call(
        _gateup_kernel,
        grid=(f // tn, m // tm),
        in_specs=[pl.BlockSpec((tm, k), lambda j, i: (i, 0)),
                  pl.BlockSpec((k, tn), lambda j, i: (0, j)),
                  pl.BlockSpec((k, tn), lambda j, i: (0, j))],
        out_specs=pl.BlockSpec((tm, tn), lambda j, i: (i, j)),
        out_shape=jax.ShapeDtypeStruct((m, f), BF16),
        scratch_shapes=[pltpu.VMEM((k, tn), BF16), pltpu.VMEM((k, tn), BF16)],
        compiler_params=_params(("arbitrary", "arbitrary"), blocks,
                                2 * _nbytes((k, tn), BF16) + 3 * _nbytes((tm, tn), F32)),
        name="ffn_gateup",
    )(a, wg, wu)


def _block_diag_ones(n, block, dtype):
    r = lax.broadcasted_iota(jnp.int32, (n, n), 0) // block
    c = lax.broadcasted_iota(jnp.int32, (n, n), 1) // block
    return jnp.where(r == c, 1.0, 0.0).astype(dtype)


def _head_sums(xs):
    x = xs[0] if len(xs) == 1 else jnp.concatenate(xs, axis=0)
    w = x.shape[1]
    gl = UNIT_LANES if w % UNIT_LANES == 0 else LANE
    bd = _block_diag_ones(gl, RW_HEAD, BF16)
    hi = x.astype(BF16)
    lo = (x - hi.astype(F32)).astype(BF16)
    parts = [_dot(hi[:, s:s + gl], bd) + _dot(lo[:, s:s + gl], bd) for s in range(0, w, gl)]
    out = parts[0] if len(parts) == 1 else jnp.concatenate(parts, axis=1)
    n = xs[0].shape[0]
    return [out[i * n:(i + 1) * n] for i in range(len(xs))]


def _rw_prep_kernel(*refs, rw, seg, has_vfirst, n_prompt, t_prompt):
    if has_vfirst:
        (p_ref, prev_ref, vf_ref, mu_ref, w2_ref, w0_ref, a2_ref, a0_ref, g2_ref, kk_ref, ka_ref,
         v2_ref, v0_ref, r_out, w_out, k_out, v_out, a_out, b_out, g_out) = refs
    else:
        (p_ref, prev_ref, mu_ref, w2_ref, w0_ref, a2_ref, a0_ref, g2_ref, kk_ref, ka_ref,
         r_out, w_out, k_out, v_out, a_out, b_out, g_out) = refs
    tr = p_ref.shape[0]
    row0 = pl.program_id(0) * tr
    rows = lax.broadcasted_iota(jnp.int32, (tr, 1), 0)
    grow = rows + row0
    seq_start = jnp.logical_and(grow < n_prompt, grow % t_prompt == 0)

    def shifted_lerp(lo, width):
        p = p_ref[:, lo:lo + width]
        above = pltpu.roll(p, 1, axis=0)
        above = jnp.where(rows == 0, prev_ref[SUBLANE - 1:SUBLANE, lo:lo + width], above)
        above = jnp.where(seq_start, 0.0, above)
        return p + mu_ref[:, lo:lo + width] * (above - p)

    o_r, o_k, o_v, o_w, w_w, o_a, w_a, o_g, w_g, o_xv, w_xv = seg
    r = shifted_lerp(o_r, rw)
    kr = shifted_lerp(o_k, rw)
    vr = shifted_lerp(o_v, rw)
    xw = shifted_lerp(o_w, w_w)
    xa = shifted_lerp(o_a, w_a)
    xg = shifted_lerp(o_g, w_g)

    w_log = -_softplus(-(w0_ref[...] + _dot(jnp.tanh(xw).astype(BF16), w2_ref[...]))) - 0.5
    decay = jnp.exp(-jnp.exp(w_log))
    a = _sigmoid(a0_ref[...] + _dot(xa.astype(BF16), a2_ref[...]))
    g = _dot(_sigmoid(xg).astype(BF16), g2_ref[...])
    if has_vfirst:
        xv = shifted_lerp(o_xv, w_xv)
        vr = vr + (vf_ref[...] - vr) * _sigmoid(v0_ref[...] + _dot(xv.astype(BF16), v2_ref[...]))
    kk = kr * kk_ref[...]
    kk = kk * lax.rsqrt(jnp.maximum(_head_sums([kk * kk])[0], 1e-24))
    r_out[...] = r
    w_out[...] = decay
    k_out[...] = kr * (1.0 + (a - 1.0) * ka_ref[...])
    v_out[...] = vr
    a_out[...] = -kk
    b_out[...] = kk * a
    g_out[...] = g


def rw_prep(p_rw, seg, rw, mu, w2, w0, a2, a0, g2, kk, ka, v2, v0, v_first, n_prompt, t_prompt):
    m, n_rw = p_rw.shape
    tr = _tile(m, 128, SUBLANE)
    nsub = tr // SUBLANE
    has_vfirst = v_first is not None
    row = lambda w: pl.BlockSpec((1, w), lambda i: (0, 0))
    full = lambda x: pl.BlockSpec(x.shape, lambda i: (0, 0))
    tile = pl.BlockSpec((tr, rw), lambda i: (i, 0))
    in_specs = [pl.BlockSpec((tr, n_rw), lambda i: (i, 0)),
                pl.BlockSpec((SUBLANE, n_rw), lambda i: (jnp.maximum(i * nsub - 1, 0), 0))]
    args = [p_rw, p_rw]
    if has_vfirst:
        in_specs.append(tile)
        args.append(v_first)
    in_specs += [row(n_rw), full(w2), row(rw), full(a2), row(rw), full(g2), row(rw), row(rw)]
    args += [mu.reshape(1, n_rw), w2, w0.reshape(1, rw), a2, a0.reshape(1, rw), g2,
             kk.reshape(1, rw), ka.reshape(1, rw)]
    if has_vfirst:
        in_specs += [full(v2), row(rw)]
        args += [v2, v0.reshape(1, rw)]
    blocks = [_nbytes((tr + SUBLANE, n_rw), F32), (7 + has_vfirst) * _nbytes((tr, rw), F32),
              sum(_nbytes(x.shape, x.dtype) for x in (w2, a2, g2)) + (_nbytes(v2.shape, v2.dtype) if has_vfirst else 0)]
    outs = pl.pallas_call(
        functools.partial(_rw_prep_kernel, rw=rw, seg=seg, has_vfirst=has_vfirst,
                          n_prompt=n_prompt, t_prompt=t_prompt),
        grid=(m // tr,),
        in_specs=in_specs,
        out_specs=[tile] * 7,
        out_shape=[jax.ShapeDtypeStruct((m, rw), F32)] * 7,
        compiler_params=_params(("arbitrary",), blocks, 12 * _nbytes((tr, rw), F32)),
        name="rwkv_prep",
    )(*args)
    return outs


def _rw_scan_kernel(*refs, nb, n_groups, tc, t_start, split_in, layer, owns_stack):
    n_op = 7 * (nb if split_in else 1)
    op_refs = refs[:n_op]
    gnw_ref, gnb_ref, rk_ref, s0_ref = refs[n_op:n_op + 4]
    o_ref, sout_ref, s_scr, lhs_scr, y8_scr, y_scr = refs[-6:]
    c = pl.program_id(1)
    units = [(s, grp) for s in range(nb) for grp in range(n_groups)]

    def operand(kind, s):
        return (op_refs[kind * nb + s], 0) if split_in else (op_refs[kind], s * tc)

    @pl.when(c == 0)
    def _():
        for u, (s, grp) in enumerate(units):
            s_scr[u * RW_HEAD:(u + 1) * RW_HEAD, :] = jnp.concatenate(
                [s0_ref[s, UNIT_HEADS * grp + h] for h in range(UNIT_HEADS)], axis=1)

    bd = _block_diag_ones(UNIT_LANES, RW_HEAD, BF16)
    diag = (lax.broadcasted_iota(jnp.int32, (RW_HEAD, UNIT_LANES), 1) % RW_HEAD
            == lax.broadcasted_iota(jnp.int32, (RW_HEAD, UNIT_LANES), 0))

    def group_step(base, first):
        def row(kind, u, j):
            s, grp = units[u]
            ref, off = operand(kind, s)
            return ref[pl.ds(off + base, SUBLANE), grp * UNIT_LANES:(grp + 1) * UNIT_LANES][j:j + 1, :]

        n_u = len(units)
        q0 = 2 * n_u * RW_HEAD

        def put_sa_v(par, u, s, j):
            lhs_scr[par, 2 * u * RW_HEAD:(2 * u + 1) * RW_HEAD, :] = (s * row(4, u, j)).astype(BF16)
            lhs_scr[par, (2 * u + 1) * RW_HEAD:(2 * u + 2) * RW_HEAD, :] = (
                jnp.where(diag, row(3, u, j), 0.0).astype(BF16))

        if first:
            y8_scr[...] = jnp.zeros(y8_scr.shape, F32)
        for u in range(n_u):
            put_sa_v(0, u, s_scr[u * RW_HEAD:(u + 1) * RW_HEAD, :], first)
        bc = _dot(lhs_scr[0, :q0], bd)
        for j in range(first, SUBLANE):
            par = (j - first + 1) % 2
            last = j == SUBLANE - 1
            for u in range(n_u):
                rows = slice(u * RW_HEAD, (u + 1) * RW_HEAD)
                s = (s_scr[rows, :] * row(1, u, j)
                     + bc[2 * u * RW_HEAD:(2 * u + 1) * RW_HEAD] * row(5, u, j)
                     + bc[(2 * u + 1) * RW_HEAD:(2 * u + 2) * RW_HEAD] * row(2, u, j))
                s_scr[rows, :] = s
                lhs_scr[par, q0 + u * RW_HEAD:q0 + (u + 1) * RW_HEAD, :] = (s * row(0, u, j)).astype(BF16)
                if not last:
                    put_sa_v(par, u, s, j + 1)
            bc = _dot(lhs_scr[par, q0:] if last else lhs_scr[par], bd)
            y_bc = bc if last else bc[q0:]
            for u in range(n_u):
                y_row = jnp.sum(jnp.where(diag, y_bc[u * RW_HEAD:(u + 1) * RW_HEAD], 0.0), axis=0, keepdims=True)
                y8_scr[u * SUBLANE + j:u * SUBLANE + j + 1, :] = y_row
        for u, (s, grp) in enumerate(units):
            y_scr[pl.ds(s * tc + base, SUBLANE), grp * UNIT_LANES:(grp + 1) * UNIT_LANES] = (
                y8_scr[u * SUBLANE:(u + 1) * SUBLANE, :])

    if tc == SUBLANE:
        group_step(0, t_start)
    else:
        assert t_start == 0 and tc % SUBLANE == 0

        def body(i, carry):
            group_step(pl.multiple_of(i * SUBLANE, SUBLANE), 0)
            return carry

        lax.fori_loop(0, tc // SUBLANE, body, 0)

    for s in range(nb):
        def full(kind):
            ref, off = operand(kind, s)
            return ref[off:off + tc, :]
        y = y_scr[s * tc:(s + 1) * tc, :]
        mean, rk_sum = _head_sums([y, full(0) * full(2) * rk_ref[...]])
        yc = y - mean * (1.0 / RW_HEAD)
        var = _head_sums([yc * yc])[0] * (1.0 / RW_HEAD)
        yn = yc * lax.rsqrt(var + GN_EPS) * gnw_ref[...] + gnb_ref[...]
        o_ref[s] = ((yn + rk_sum * full(3)) * full(6)).astype(o_ref.dtype)

    @pl.when(c == pl.num_programs(1) - 1)
    def _():
        out = _state_out(sout_ref, layer, owns_stack)
        for u, (s, grp) in enumerate(units):
            st = s_scr[u * RW_HEAD:(u + 1) * RW_HEAD, :]
            for h in range(UNIT_HEADS):
                out[s, UNIT_HEADS * grp + h] = st[:, h * RW_HEAD:(h + 1) * RW_HEAD]


def _state_io(s0, s0_layer, layer, depth, s_prev, nb, n_inputs):
    blk = (None, nb) + s0.shape[2:]
    zeros = (0,) * (s0.ndim - 2)
    in_spec = pl.BlockSpec(blk, lambda i, c: (s0_layer, i) + zeros)
    out_shape = jax.ShapeDtypeStruct((depth,) + s0.shape[1:], s0.dtype)
    if s_prev is None:
        out_spec = pl.BlockSpec((depth, nb) + s0.shape[2:], lambda i, c: (0, i) + zeros)
        return in_spec, out_spec, out_shape, [], [], {}
    out_spec = pl.BlockSpec(blk, lambda i, c: (layer, i) + zeros)
    return in_spec, out_spec, out_shape, [pl.BlockSpec(memory_space=pl.ANY)], [s_prev], {n_inputs: 1}


def _state_out(sout_ref, layer, owns_stack):
    if not owns_stack:
        return sout_ref
    for other in range(sout_ref.shape[0]):
        if other != layer:
            sout_ref[other] = jnp.zeros(sout_ref.shape[1:], sout_ref.dtype)
    return sout_ref.at[layer]


def rw_scan(ops, g, gn_w, gn_b, rk, s0, s0_layer, layer, depth, s_prev, row0, n_seq, t_seq, tc, t_start, nb):
    rw = ops[0].shape[1]
    n_heads = s0.shape[2]
    assert n_heads % UNIT_HEADS == 0
    n_groups = n_heads // UNIT_HEADS
    n_chunks = t_seq // tc
    split_in = n_chunks > 1
    assert n_seq % nb == 0 and (split_in or (row0 % (nb * tc) == 0 and tc == t_seq))
    if split_in:
        base = row0 // tc
        in_specs = [pl.BlockSpec((tc, rw), lambda i, c, s=s: (base + (i * nb + s) * n_chunks + c, 0))
                    for _ in range(7) for s in range(nb)]
        args = [x for x in (*ops, g) for _ in range(nb)]
    else:
        base = row0 // (nb * tc)
        in_specs = [pl.BlockSpec((nb * tc, rw), lambda i, c: (base + i, 0))] * 7
        args = [*ops, g]
    row = pl.BlockSpec((1, rw), lambda i, c: (0, 0))
    st_in, st_out, st_shape, prev_spec, prev_arg, alias = _state_io(
        s0, s0_layer, layer, depth, s_prev, nb, len(args) + 4)
    n_units = nb * n_groups
    scratch = [pltpu.VMEM((n_units * RW_HEAD, UNIT_LANES), F32),
               pltpu.VMEM((2, 3 * n_units * RW_HEAD, UNIT_LANES), BF16),
               pltpu.VMEM((n_units * SUBLANE, UNIT_LANES), F32),
               pltpu.VMEM((nb * tc, rw), F32)]
    blocks = [7 * _nbytes((nb * tc, rw), F32), _nbytes((nb * tc, rw), BF16),
              (1 + depth) * _nbytes((nb * n_heads, RW_HEAD, LANE), F32)]
    o, s_out = pl.pallas_call(
        functools.partial(_rw_scan_kernel, nb=nb, n_groups=n_groups, tc=tc, t_start=t_start, split_in=split_in,
                          layer=layer, owns_stack=s_prev is None),
        grid=(n_seq // nb, n_chunks),
        in_specs=in_specs + [row] * 3 + [st_in] + prev_spec,
        out_specs=[pl.BlockSpec((nb, tc, rw), lambda i, c: (i, c, 0)), st_out],
        out_shape=[jax.ShapeDtypeStruct((n_seq, t_seq, rw), BF16), st_shape],
        input_output_aliases=alias,
        scratch_shapes=scratch,
        compiler_params=_params(("arbitrary", "arbitrary"), blocks,
                                16 * _nbytes((n_units * RW_HEAD, UNIT_LANES), F32) + 8 * _nbytes((tc, rw), F32)),
        name="rwkv_scan",
    )(*args, gn_w.reshape(1, rw), gn_b.reshape(1, rw), rk.reshape(1, rw), s0, *prev_arg)
    return o.reshape(n_seq * t_seq, rw), s_out


def _gla_kernel(*refs, n_heads, dk, dv, t_start, layer, owns_stack):
    q_ref, k_ref, v_ref, go_ref, lr_ref, w2_ref, gb_ref, gn_ref, s0_ref = refs[:9]
    o_ref, sout_ref, s_scr = refs[-3:]
    c_rows = q_ref.shape[0]
    n = pl.program_id(1)

    @pl.when(n == 0)
    def _():
        s_scr[...] = s0_ref[0]

    rows = lax.broadcasted_iota(jnp.int32, (c_rows, 1), 0)
    valid = rows >= t_start
    tri = (lax.broadcasted_iota(jnp.int32, (c_rows, c_rows), 0)
           >= lax.broadcasted_iota(jnp.int32, (c_rows, c_rows), 1))
    tri_f = jnp.where(tri, 1.0, 0.0).astype(F32)
    lr = lr_ref[...]
    mid = max(c_rows // 2 - 1, 0)
    scale = dk ** -0.5
    for h in range(n_heads):
        ks = slice(h * dk, (h + 1) * dk)
        vs = slice(h * dv, (h + 1) * dv)
        x = _dot(lr, w2_ref[:, ks], precision=HIGHEST) + gb_ref[:, ks]
        log_a = (jnp.minimum(x, 0.0) - jnp.log(1.0 + jnp.exp(-jnp.abs(x)))) * (1.0 / GLA_GATE_TAU)
        log_a = jnp.where(valid, log_a, 0.0)
        b = _dot(tri_f, log_a, precision=HIGHEST)
        b_last = b[c_rows - 1:c_rows, :]
        b_mid = b[mid:mid + 1, :]
        q = q_ref[:, ks] * scale
        k = jnp.where(valid, k_ref[:, ks], 0.0)
        v = v_ref[:, vs].astype(BF16)
        qi = (q * jnp.exp(b - b_mid)).astype(BF16)
        ki = (k * jnp.exp(b_mid - b)).astype(BF16)
        att = lax.dot_general(qi, ki, (((1,), (1,)), ((), ())), preferred_element_type=F32)
        att = jnp.where(tri, att, 0.0).astype(BF16)
        s = s_scr[h]
        o = _dot(att, v) + _dot((q * jnp.exp(b)).astype(BF16), s.astype(BF16))
        kd = (k * jnp.exp(b_last - b)).astype(BF16)
        upd = lax.dot_general(kd, v, (((0,), (0,)), ((), ())), preferred_element_type=F32)
        dcol = jnp.transpose(jnp.broadcast_to(jnp.exp(b_last), (LANE, dk)))
        s_scr[h] = s * jnp.concatenate([dcol] * (dv // LANE), axis=1) + upd
        on = o * lax.rsqrt(jnp.mean(o * o, axis=-1, keepdims=True) + HEAD_NORM_EPS)
        go = go_ref[:, vs]
        o_ref[:, vs] = (on * gn_ref[:, vs] * (go * _sigmoid(go))).astype(o_ref.dtype)

    @pl.when(n == pl.num_programs(1) - 1)
    def _():
        _state_out(sout_ref, layer, owns_stack)[0] = s_scr[...]


def gla(p_gla, p_rw, lr_block, w2p, gb, gn, s0, s0_layer, layer, depth, s_prev, row0, n_seq, t_seq, c_rows, t_start):
    _, _, n_heads, dk, dv = s0.shape
    qk, gw = n_heads * dk, n_heads * dv
    n_chunks = t_seq // c_rows
    base = row0 // c_rows
    rmap = lambda col: (lambda i, n: (base + i * n_chunks + n, col))
    st_in, st_out, st_shape, prev_spec, prev_arg, alias = _state_io(s0, s0_layer, layer, depth, s_prev, 1, 9)
    const = lambda shape: pl.BlockSpec(shape, lambda i, n: (0, 0))
    blocks = [_nbytes((c_rows, 2 * qk + 2 * gw + LANE), F32), _nbytes((LANE + 2, qk), F32),
              _nbytes((c_rows, gw), BF16), (1 + depth) * _nbytes((n_heads, dk, dv), F32)]
    o, s_out = pl.pallas_call(
        functools.partial(_gla_kernel, n_heads=n_heads, dk=dk, dv=dv, t_start=t_start,
                          layer=layer, owns_stack=s_prev is None),
        grid=(n_seq, n_chunks),
        in_specs=[pl.BlockSpec((c_rows, qk), rmap(0)),
                  pl.BlockSpec((c_rows, qk), rmap(1)),
                  pl.BlockSpec((c_rows, gw), rmap(2 * qk // gw)),
                  pl.BlockSpec((c_rows, gw), rmap(2 * qk // gw + 1)),
                  pl.BlockSpec((c_rows, LANE), rmap(lr_block)),
                  const((LANE, qk)), const((1, qk)), const((1, gw)), st_in] + prev_spec,
        out_specs=[pl.BlockSpec((c_rows, gw), lambda i, n: (i * n_chunks + n, 0)), st_out],
        out_shape=[jax.ShapeDtypeStruct((n_seq * t_seq, gw), BF16), st_shape],
        input_output_aliases=alias,
        scratch_shapes=[pltpu.VMEM((n_heads, dk, dv), F32)],
        compiler_params=_params(("arbitrary", "arbitrary"), blocks,
                                _nbytes((n_heads, dk, dv), F32) + 6 * _nbytes((dk, dv), F32)),
        name="gla",
    )(p_gla, p_gla, p_gla, p_gla, p_rw, w2p, gb.reshape(1, qk), gn.reshape(1, gw), s0, *prev_arg)
    return o, s_out


def _router_kernel(x_ref, g_ref, wr_ref, h_ref, info_ref, *, n_experts):
    x = x_ref[...]
    h = x * lax.rsqrt(jnp.mean(x * x, axis=-1, keepdims=True) + NORM_EPS) * g_ref[...]
    half = h.shape[1] // 2
    lo = lax.bitcast_convert_type(h[:, :half].astype(BF16).astype(F32), jnp.uint32)
    hi = lax.bitcast_convert_type(h[:, half:].astype(BF16).astype(F32), jnp.uint32)
    h_ref[...] = (lo >> 16) | (hi & jnp.uint32(0xFFFF0000))
    logits = _dot(h, wr_ref[...], precision=HIGHEST)
    lane = lax.broadcasted_iota(jnp.int32, logits.shape, 1)
    neg = jnp.float32(-jnp.inf)
    logits = jnp.where(lane < n_experts, logits, neg)
    m1 = jnp.max(logits, axis=-1, keepdims=True)
    i1 = jnp.min(jnp.where(logits == m1, lane, LANE), axis=-1, keepdims=True)
    rest = jnp.where(lane == i1, neg, logits)
    m2 = jnp.max(rest, axis=-1, keepdims=True)
    i2 = jnp.min(jnp.where(rest == m2, lane, LANE), axis=-1, keepdims=True)
    e2 = jnp.exp(m2 - m1)
    g1 = 1.0 / (1.0 + e2)
    g2 = e2 / (1.0 + e2)
    info = jnp.where(lane == 0, i1.astype(F32),
                     jnp.where(lane == 1, i2.astype(F32),
                               jnp.where(lane == 2, g1, jnp.where(lane == 3, g2, 0.0))))
    info_ref[...] = info


def moe_router(x, g, router):
    m, d = x.shape
    n_experts = router.shape[1]
    tr = _tile(m, 256, SUBLANE)
    wr = jnp.zeros((d, LANE), F32).at[:, :n_experts].set(router)
    blocks = [_nbytes((tr, d), F32), _nbytes((d, LANE), F32), _nbytes((tr, d // 2), jnp.uint32)]
    return pl.pallas_call(
        functools.partial(_router_kernel, n_experts=n_experts),
        grid=(m // tr,),
        in_specs=[pl.BlockSpec((tr, d), lambda i: (i, 0)), pl.BlockSpec((1, d), lambda i: (0, 0)),
                  pl.BlockSpec((d, LANE), lambda i: (0, 0))],
        out_specs=[pl.BlockSpec((tr, d // 2), lambda i: (i, 0)), pl.BlockSpec((tr, LANE), lambda i: (i, 0))],
        out_shape=[jax.ShapeDtypeStruct((m, d // 2), jnp.uint32), jax.ShapeDtypeStruct((m, LANE), F32)],
        compiler_params=_params(("arbitrary",), blocks, 4 * _nbytes((tr, d), F32)),
        name="moe_router",
    )(x, g.reshape(1, d), wr)


def _gather_kernel(idx_ref, src_ref, out_ref, buf, sem):
    tg, w = buf.shape

    def copy(r):
        return pltpu.make_async_copy(src_ref.at[pl.ds(idx_ref[0, 0, r], 1)], buf.at[pl.ds(r, 1)], sem)

    def start(r, carry):
        copy(r).start()
        return carry

    def wait(r, carry):
        copy(r).wait()
        return carry

    lax.fori_loop(0, tg, start, 0)
    lax.fori_loop(0, tg, wait, 0)
    words = buf[...]
    out_ref[:, :w] = lax.bitcast_convert_type(words << 16, F32).astype(BF16)
    out_ref[:, w:] = lax.bitcast_convert_type(words & jnp.uint32(0xFFFF0000), F32).astype(BF16)


def gather_rows(src, idx, tg=256):
    n = idx.shape[0]
    tg = _tile(n, tg, 16)
    w = src.shape[1]
    return pl.pallas_call(
        _gather_kernel,
        grid=(n // tg,),
        in_specs=[pl.BlockSpec((1, 1, tg), lambda i: (i, 0, 0), memory_space=pltpu.SMEM),
                  pl.BlockSpec(memory_space=pl.ANY)],
        out_specs=pl.BlockSpec((tg, 2 * w), lambda i: (i, 0)),
        out_shape=jax.ShapeDtypeStruct((n, 2 * w), BF16),
        scratch_shapes=[pltpu.VMEM((tg, w), src.dtype), pltpu.SemaphoreType.DMA(())],
        compiler_params=_params(("arbitrary",), [_nbytes((tg, 2 * w), BF16)], 4 * _nbytes((tg, w), src.dtype)),
        name="moe_gather",
    )(idx.reshape(n // tg, 1, tg), src)


def _moe_gateup_kernel(te_ref, tf_ref, tv_ref, ts_ref, a_ref, wg_ref, wu_ref, o_ref, wgb_ref, wub_ref):
    t = pl.program_id(1)

    @pl.when(tf_ref[t] == 1)
    def _():
        wgb_ref[...] = wg_ref[...].astype(BF16)
        wub_ref[...] = wu_ref[...].astype(BF16)

    @pl.when(tv_ref[t] == 1)
    def _():
        a = a_ref[...]
        g = _dot(a, wgb_ref[...])
        u = _dot(a, wub_ref[...])
        o_ref[...] = (g * _sigmoid(g) * u).astype(o_ref.dtype)

    @pl.when(tv_ref[t] == 0)
    def _():
        o_ref[...] = jnp.zeros(o_ref.shape, o_ref.dtype)


def moe_gateup(xs, wg, wu, tile_expert, tile_first, tile_valid, tile_src, tm, tn_pref=512):
    r, d = xs.shape
    f = wg.shape[2]
    tn = _tile(f, tn_pref, LANE)
    wspec = pl.BlockSpec((None, d, tn), lambda j, t, te, tf, tv, ts: (te[t], 0, j))
    blocks = [_nbytes((tm, d), BF16), 2 * _nbytes((d, tn), F32), _nbytes((tm, tn), BF16)]
    return pl.pallas_call(
        _moe_gateup_kernel,
        grid_spec=pltpu.PrefetchScalarGridSpec(
            num_scalar_prefetch=4,
            grid=(f // tn, r // tm),
            in_specs=[pl.BlockSpec((tm, d), lambda j, t, te, tf, tv, ts: (ts[t], 0)), wspec, wspec],
            out_specs=pl.BlockSpec((tm, tn), lambda j, t, te, tf, tv, ts: (t, j)),
            scratch_shapes=[pltpu.VMEM((d, tn), BF16), pltpu.VMEM((d, tn), BF16)]),
        out_shape=jax.ShapeDtypeStruct((r, f), BF16),
        compiler_params=_params(("arbitrary", "arbitrary"), blocks,
                                2 * _nbytes((d, tn), BF16) + 3 * _nbytes((tm, tn), F32)),
        name="moe_gateup",
    )(tile_expert, tile_first, tile_valid, tile_src, xs, wg, wu)


def _moe_down_kernel(te_ref, tf_ref, tv_ref, ts_ref, a_ref, w_hbm, o_ref, wb_ref, stage, sem, *, nk, tk, tn):
    j = pl.program_id(0)
    t = pl.program_id(1)

    def chunk_copy(c):
        src = w_hbm.at[te_ref[t], pl.ds(c * tk, tk), pl.ds(pl.multiple_of(j * tn, LANE), tn)]
        return pltpu.make_async_copy(src, stage.at[c % 2], sem.at[c % 2])

    def chunk_dot(c):
        return _dot(a_ref[:, c * tk:(c + 1) * tk], wb_ref[c])

    @pl.when(tf_ref[t] == 1)
    def _():
        chunk_copy(0).start()
        acc = None
        for c in range(nk):
            if c + 1 < nk:
                chunk_copy(c + 1).start()
            chunk_copy(c).wait()
            wb_ref[c] = stage[c % 2].astype(BF16)
            acc = chunk_dot(c) if acc is None else acc + chunk_dot(c)
        o_ref[...] = acc

    @pl.when(jnp.logical_and(tf_ref[t] == 0, tv_ref[t] == 1))
    def _():
        acc = chunk_dot(0)
        for c in range(1, nk):
            acc = acc + chunk_dot(c)
        o_ref[...] = acc

    @pl.when(tv_ref[t] == 0)
    def _():
        o_ref[...] = jnp.zeros(o_ref.shape, o_ref.dtype)


def moe_down(act, wd, tile_expert, tile_first, tile_valid, tile_src, tm, tn_pref=512, tk_pref=1792):
    r, f = act.shape
    d = wd.shape[2]
    tn = _tile(d, tn_pref, LANE)
    tk = _tile(f, tk_pref, LANE)
    nk = f // tk
    blocks = [_nbytes((tm, f), BF16), _nbytes((tm, tn), F32)]
    return pl.pallas_call(
        functools.partial(_moe_down_kernel, nk=nk, tk=tk, tn=tn),
        grid_spec=pltpu.PrefetchScalarGridSpec(
            num_scalar_prefetch=4,
            grid=(d // tn, r // tm),
            in_specs=[pl.BlockSpec((tm, f), lambda j, t, te, tf, tv, ts: (ts[t], 0)),
                      pl.BlockSpec(memory_space=pl.ANY)],
            out_specs=pl.BlockSpec((tm, tn), lambda j, t, te, tf, tv, ts: (t, j)),
            scratch_shapes=[pltpu.VMEM((nk, tk, tn), BF16), pltpu.VMEM((2, tk, tn), F32),
                            pltpu.SemaphoreType.DMA((2,))]),
        out_shape=jax.ShapeDtypeStruct((r, d), F32),
        compiler_params=_params(("arbitrary", "arbitrary"), blocks,
                                _nbytes((f, tn), BF16) + 2 * _nbytes((tk, tn), F32) + 2 * _nbytes((tm, tn), F32)),
        name="moe_down",
    )(tile_expert, tile_first, tile_valid, tile_src, act, wd)


def _combine_kernel(p1_ref, p2_ref, x_ref, info_ref, ys_ref, o_ref, buf1, buf2, sem):
    tc = x_ref.shape[0]

    def copies(r):
        return (pltpu.make_async_copy(ys_ref.at[pl.ds(p1_ref[0, 0, r], 1)], buf1.at[pl.ds(r, 1)], sem.at[0]),
                pltpu.make_async_copy(ys_ref.at[pl.ds(p2_ref[0, 0, r], 1)], buf2.at[pl.ds(r, 1)], sem.at[1]))

    def start(r, carry):
        c1, c2 = copies(r)
        c1.start()
        c2.start()
        return carry

    def wait(r, carry):
        c1, c2 = copies(r)
        c1.wait()
        c2.wait()
        return carry

    lax.fori_loop(0, tc, start, 0)
    lax.fori_loop(0, tc, wait, 0)
    info = info_ref[...]
    o_ref[...] = x_ref[...] + info[:, 2:3] * buf1[...] + info[:, 3:4] * buf2[...]


def moe_combine(x, info, ys, pos1, pos2, tc=128):
    m, d = x.shape
    tc = _tile(m, tc, SUBLANE)
    idx = pl.BlockSpec((1, 1, tc), lambda i: (i, 0, 0), memory_space=pltpu.SMEM)
    blocks = [2 * _nbytes((tc, d), F32), _nbytes((tc, LANE), F32)]
    return pl.pallas_call(
        _combine_kernel,
        grid=(m // tc,),
        in_specs=[idx, idx, pl.BlockSpec((tc, d), lambda i: (i, 0)), pl.BlockSpec((tc, LANE), lambda i: (i, 0)),
                  pl.BlockSpec(memory_space=pl.ANY)],
        out_specs=pl.BlockSpec((tc, d), lambda i: (i, 0)),
        out_shape=jax.ShapeDtypeStruct((m, d), F32),
        scratch_shapes=[pltpu.VMEM((tc, d), F32), pltpu.VMEM((tc, d), F32), pltpu.SemaphoreType.DMA((2,))],
        compiler_params=_params(("arbitrary",), blocks, 4 * _nbytes((tc, d), F32)),
        name="moe_combine",
    )(pos1.reshape(m // tc, 1, tc), pos2.reshape(m // tc, 1, tc), x, info, ys)


def moe_ffn(x, norm_g, router, wg, wu, wd, row_valid, n_valid, tm=384):
    m, d = x.shape
    n_experts = router.shape[1]
    h_packed, info = moe_router(x, norm_g, router)

    none = n_experts
    e1 = jnp.where(row_valid, info[:, 0].astype(jnp.int32), none)
    e2 = jnp.where(row_valid, info[:, 1].astype(jnp.int32), none)
    e_all = jnp.concatenate([e1, e2])
    onehot = (e_all[:, None] == jnp.arange(n_experts, dtype=jnp.int32)[None, :]).astype(jnp.int32)
    counts = jnp.sum(onehot, axis=0)
    rank = jnp.sum((jnp.cumsum(onehot, axis=0) - onehot) * onehot, axis=1)
    padded = (counts + tm - 1) // tm * tm
    ends = jnp.cumsum(padded)
    starts = ends - padded
    n_rows = _round_up(2 * n_valid + n_experts * (tm - 1), tm)
    n_tiles = n_rows // tm
    pos = jnp.where(e_all < none, jnp.take(starts, jnp.minimum(e_all, none - 1)) + rank, n_rows)
    tok = jnp.tile(jnp.arange(m, dtype=jnp.int32), 2)
    row_src = jnp.zeros((n_rows,), jnp.int32).at[pos].set(tok, mode="drop")
    tile_start = jnp.arange(n_tiles, dtype=jnp.int32) * tm
    tile_valid = (tile_start < ends[-1]).astype(jnp.int32)
    last_tile = jnp.maximum(ends[-1] // tm - 1, 0)
    tile_expert = jnp.searchsorted(ends, jnp.minimum(tile_start, last_tile * tm), side="right").astype(jnp.int32)
    tile_expert = jnp.minimum(tile_expert, n_experts - 1)
    prev_expert = jnp.concatenate([jnp.full((1,), -1, jnp.int32), tile_expert[:-1]])
    tile_first = jnp.logical_and(tile_expert != prev_expert, tile_valid == 1).astype(jnp.int32)
    tile_first = tile_first.at[0].set(1)
    tile_src = jnp.minimum(jnp.arange(n_tiles, dtype=jnp.int32), last_tile).astype(jnp.int32)
    pos_safe = jnp.where(pos < n_rows, pos, 0)
    gates = jnp.where(row_valid[:, None], info, 0.0)

    xs = gather_rows(h_packed, row_src)
    act = moe_gateup(xs, wg, wu, tile_expert, tile_first, tile_valid, tile_src, tm)
    ys = moe_down(act, wd, tile_expert, tile_first, tile_valid, tile_src, tm)
    return moe_combine(x, gates, ys, pos_safe[:m], pos_safe[m:])


def _pad_cols(w, width):
    return w if w.shape[-1] == width else jnp.pad(w, [(0, 0)] * (w.ndim - 1) + [(0, width - w.shape[-1])])


def _pad_rows(w, rows):
    return w if w.shape[0] == rows else jnp.pad(w, [(0, rows - w.shape[0])] + [(0, 0)] * (w.ndim - 1))


def kernel(x_prompt, x_sample, state_gla, state_rwkv, state_shift, norm_mix, norm_ffn, norm_final,
           w_in_first, w_in_rest, mu_first, mu_rest, gla_w2, gla_b, gla_norm,
           rw_w2, rw_w0, rw_a2, rw_a0, rw_g2, rw_kk, rw_ka, rw_rk, rw_gn_w, rw_gn_b, rw_v2, rw_v0,
           w_out, ffn_w_gate, ffn_w_up, ffn_w_down, moe_router_w, moe_w_gate, moe_w_up, moe_w_down):
    batch, seq, d = x_prompt.shape
    dec_batch, dec_seq, _ = x_sample.shape
    depth = norm_mix.shape[0]
    _, _, gla_heads, dk, dv = state_gla.shape
    rw_heads = state_rwkv.shape[2]
    assert state_rwkv.shape[3] == RW_HEAD and state_rwkv.shape[4] == RW_HEAD and rw_heads % 2 == 0
    qk, gw, rw = gla_heads * dk, gla_heads * dv, rw_heads * RW_HEAD
    rank = gla_w2.shape[1]
    p_gla = 2 * qk + 2 * gw + rank
    assert dk % LANE == 0 and dv % LANE == 0 and (2 * qk) % gw == 0 and rank <= LANE

    group = _round_up(dec_seq + 1, SUBLANE)
    t_start = group - dec_seq
    n_prompt = batch * seq
    m = n_prompt + dec_batch * group
    prompt_chunk = _tile(seq, 64, SUBLANE)
    scan_chunk = _tile(seq, 64, SUBLANE)
    sample_rows = jnp.arange(dec_batch * group, dtype=jnp.int32) % group >= t_start
    row_valid = jnp.concatenate([jnp.ones((n_prompt,), bool), sample_rows])

    x = jnp.concatenate([
        x_prompt.reshape(n_prompt, d),
        jnp.concatenate([jnp.zeros((dec_batch, t_start, d), F32), x_sample], axis=1).reshape(dec_batch * group, d)])

    lw, la, lg = rw_w2.shape[1], rw_a2.shape[1], rw_g2.shape[1]
    lv = rw_v2.shape[1]
    pw, pa, pg, pv = (_round_up(n, LANE) for n in (lw, la, lg, lv))
    o_w = 3 * rw
    o_a = o_w + pw
    o_g = o_a + pa
    o_xv = o_g + pg
    o_lr = o_xv + pv
    n_rw = _round_up(o_lr + LANE, 512)
    seg = (0, rw, 2 * rw, o_w, pw, o_a, pa, o_g, pg, o_xv, pv)

    def repack(wt, mu_l, has_v):
        wr = wt[p_gla:]
        parts_w, parts_mu = [], []
        src = 0
        for width, padded in ((3 * rw, 3 * rw), (lw, pw), (la, pa), (lg, pg)) + (((lv, pv),) if has_v else ()):
            parts_w.append(_pad_rows(wr[src:src + width], padded))
            parts_mu.append(_pad_cols(mu_l[src:src + width], padded))
            src += width
        if not has_v:
            parts_w.append(jnp.zeros((pv, d), F32))
            parts_mu.append(jnp.zeros((pv,), F32))
        parts_w.append(_pad_rows(wt[p_gla - rank:p_gla], n_rw - o_lr))
        parts_mu.append(jnp.zeros((n_rw - o_lr,), F32))
        return jnp.concatenate(parts_w, axis=0).astype(BF16), jnp.concatenate(parts_mu)

    zeros_gla = jnp.zeros((1, batch, gla_heads, dk, dv), F32)
    zeros_rw = jnp.zeros((1, batch, rw_heads, RW_HEAD, RW_HEAD), F32)
    last_rows = jnp.concatenate([jnp.arange(batch, dtype=jnp.int32) * seq + seq - 1,
                                 n_prompt + jnp.arange(dec_batch, dtype=jnp.int32) * group + group - 1])
    n_last = _round_up(batch + dec_batch, SUBLANE)
    prev_rows = n_prompt + jnp.arange(dec_batch, dtype=jnp.int32) * group + t_start - 1

    new_gla_p = new_gla_s = new_rw_p = new_rw_s = None
    new_shift = []
    v_first = None
    for l in range(depth):
        h = rmsnorm(x, norm_mix[l], BF16).at[prev_rows].set(state_shift[l].astype(BF16))
        x_last = _pad_rows(jnp.take(x, last_rows, axis=0), n_last)
        new_shift.append(rmsnorm(x_last, norm_mix[l], F32)[:batch + dec_batch])

        if l == 0:
            wt, mu_l = w_in_first.T, mu_first
        else:
            wt, mu_l = w_in_rest[l - 1].T, mu_rest[l - 1]
        w_rw, mu_p = repack(wt, mu_l, l > 0)
        pg_all = matmul([h], wt, 2 * qk + 2 * gw, w_t=True, name="w_in_gla")
        pr_all = matmul([h], w_rw, n_rw, w_t=True, name="w_in_rwkv")

        ops = rw_prep(pr_all, seg, rw, mu_p,
                      _pad_rows(rw_w2[l], pw).astype(BF16), rw_w0[l], _pad_rows(rw_a2[l], pa).astype(BF16), rw_a0[l],
                      _pad_rows(rw_g2[l], pg).astype(BF16), rw_kk[l], rw_ka[l],
                      _pad_rows(rw_v2[l - 1], pv).astype(BF16) if l > 0 else None,
                      rw_v0[l - 1] if l > 0 else None, v_first, n_prompt, seq)
        r_, w_, k_, v_, a_, b_, g_ = ops
        if l == 0:
            v_first = v_
        scan_args = ((r_, w_, k_, v_, a_, b_), g_, rw_gn_w[l], rw_gn_b[l], rw_rk[l].reshape(rw))
        o_rw_p, new_rw_p = rw_scan(*scan_args, zeros_rw, 0, l, depth, new_rw_p,
                                   0, batch, seq, scan_chunk, 0, _tile(batch, 4, 1))
        o_rw_s, new_rw_s = rw_scan(*scan_args, state_rwkv, l, l, depth, new_rw_s,
                                   n_prompt, dec_batch, group, group, t_start, _tile(dec_batch, 4, 1))

        w2p = _pad_rows(gla_w2[l], LANE)
        gla_args = (pg_all, pr_all, o_lr // LANE, w2p, gla_b[l], gla_norm[l])
        o_gla_p, new_gla_p = gla(*gla_args, zeros_gla, 0, l, depth, new_gla_p, 0, batch, seq, prompt_chunk, 0)
        o_gla_s, new_gla_s = gla(*gla_args, state_gla, l, l, depth, new_gla_s,
                                 n_prompt, dec_batch, group, group, t_start)

        o_gla = jnp.concatenate([o_gla_p, o_gla_s])
        o_rw = jnp.concatenate([o_rw_p, o_rw_s])
        x = matmul([o_gla, o_rw], w_out, d, res=x, layer=l, name="w_out")

        i = l // 2
        if l % 2 == 0:
            h2 = rmsnorm(x, norm_ffn[l], BF16)
            act = gateup(h2, ffn_w_gate[i], ffn_w_up[i])
            x = matmul([act], ffn_w_down[i].astype(BF16), d, res=x, tm_pref=512, name="ffn_down")
        else:
            x = moe_ffn(x, norm_ffn[l], moe_router_w[i], moe_w_gate[i], moe_w_up[i], moe_w_down[i],
                        row_valid, n_prompt + dec_batch * dec_seq)

    y_prompt = rmsnorm(x, norm_final, F32, 0, n_prompt).reshape(batch, seq, d)
    y_sample = rmsnorm(x, norm_final, F32, n_prompt, m - n_prompt).reshape(dec_batch, group, d)[:, t_start:]
    shift = jnp.stack(new_shift)
    return (y_prompt, y_sample, new_gla_p, new_gla_s, new_rw_p, new_rw_s, shift[:, :batch], shift[:, batch:])
```

```python
import functools
import math

import jax
import jax.numpy as jnp
from jax import lax
from jax.experimental import pallas as pl
from jax.experimental.pallas import tpu as pltpu

F32 = jnp.float32
BF16 = jnp.bfloat16
HIGHEST = lax.Precision.HIGHEST

NORM_EPS = 1e-6
HEAD_NORM_EPS = 1e-5
GN_EPS = 64e-5
GLA_GATE_TAU = 16.0

LANE = 128
SUBLANE = 8
RW_HEAD = 64
MXU_DIM = 256
UNIT_HEADS = MXU_DIM // RW_HEAD
UNIT_LANES = MXU_DIM
VMEM_CAP = 64 << 20
VMEM_BUDGET = VMEM_CAP - (6 << 20)


def _round_up(n, m):
    return (n + m - 1) // m * m


def _tile(n, pref, align):
    t = min(pref, n)
    t -= t % align
    while t >= align:
        if n % t == 0:
            return t
        t -= align
    return n


def _nbytes(shape, dtype):
    return math.prod(shape) * jnp.dtype(dtype).itemsize


def _params(sem, blocks, scratch=0):
    need = 2 * sum(blocks) + scratch
    limit = min(VMEM_BUDGET, max(need + need // 4, 16 << 20))
    return pltpu.CompilerParams(dimension_semantics=sem, vmem_limit_bytes=int(limit))


def _sigmoid(x):
    return 1.0 / (1.0 + jnp.exp(-x))


def _softplus(x):
    return jnp.maximum(x, 0.0) + jnp.log(1.0 + jnp.exp(-jnp.abs(x)))


def _dot(a, b, **kw):
    return jnp.dot(a, b, preferred_element_type=F32, **kw)


def _rmsnorm_kernel(x_ref, g_ref, o_ref):
    x = x_ref[...]
    y = x * lax.rsqrt(jnp.mean(x * x, axis=-1, keepdims=True) + NORM_EPS)
    o_ref[...] = (y * g_ref[...]).astype(o_ref.dtype)


def rmsnorm(x, g, out_dtype, row0=0, rows=None):
    d = x.shape[1]
    m = x.shape[0] if rows is None else rows
    tr = _tile(math.gcd(m, row0) if row0 else m, 256, SUBLANE)
    first = row0 // tr
    return pl.pallas_call(
        _rmsnorm_kernel,
        grid=(m // tr,),
        in_specs=[pl.BlockSpec((tr, d), lambda i: (first + i, 0)), pl.BlockSpec((1, d), lambda i: (0, 0))],
        out_specs=pl.BlockSpec((tr, d), lambda i: (i, 0)),
        out_shape=jax.ShapeDtypeStruct((m, d), out_dtype),
        compiler_params=_params(("arbitrary",), [_nbytes((tr, d), F32), _nbytes((tr, d), out_dtype)]),
        name="rmsnorm",
    )(x, g.reshape(1, d))


def _mm_kernel(*refs, n_a, has_res, cast_w, w_t):
    a_refs = refs[:n_a]
    w_ref = refs[n_a]
    res_ref = refs[n_a + 1] if has_res else None
    o_ref = refs[n_a + 1 + has_res]
    if cast_w:
        wb_ref = refs[n_a + 2 + has_res]

        @pl.when(pl.program_id(1) == 0)
        def _():
            wb_ref[...] = w_ref[...].astype(BF16)
        w_ref = wb_ref
    acc = None
    off = 0
    for a_ref in a_refs:
        ka = a_ref.shape[1]
        if w_t:
            part = lax.dot_general(a_ref[...], w_ref[:, off:off + ka], (((1,), (1,)), ((), ())),
                                   preferred_element_type=F32)
        else:
            part = _dot(a_ref[...], w_ref[off:off + ka, :])
        acc = part if acc is None else acc + part
        off += ka
    if has_res:
        acc = acc + res_ref[...]
    o_ref[...] = acc.astype(o_ref.dtype)


def matmul(a_list, w, n_out, res=None, out_dtype=F32, tm_pref=1024, tn_pref=512, layer=None, w_t=False,
           name="matmul"):
    m = a_list[0].shape[0]
    k = w.shape[-1] if w_t else w.shape[-2]
    assert sum(a.shape[1] for a in a_list) == k and (w.ndim == 2) == (layer is None) and not (w_t and layer)
    tm = _tile(m, tm_pref, 16)
    tn = _tile(n_out, tn_pref, LANE)
    cast_w = w.dtype != BF16
    in_specs = [pl.BlockSpec((tm, a.shape[1]), lambda j, i: (i, 0)) for a in a_list]
    if w_t:
        in_specs.append(pl.BlockSpec((tn, k), lambda j, i: (j, 0)))
    elif layer is None:
        in_specs.append(pl.BlockSpec((k, tn), lambda j, i: (0, j)))
    else:
        in_specs.append(pl.BlockSpec((None, k, tn), lambda j, i: (layer, 0, j)))
    args = list(a_list) + [w]
    blocks = [_nbytes((tm, k), BF16), _nbytes((k, tn), w.dtype), _nbytes((tm, tn), out_dtype)]
    if res is not None:
        in_specs.append(pl.BlockSpec((tm, tn), lambda j, i: (i, j)))
        args.append(res)
        blocks.append(_nbytes((tm, tn), F32))
    scratch = [pltpu.VMEM((tn, k) if w_t else (k, tn), BF16)] if cast_w else []
    return pl.pallas_call(
        functools.partial(_mm_kernel, n_a=len(a_list), has_res=res is not None, cast_w=cast_w, w_t=w_t),
        grid=(n_out // tn, m // tm),
        in_specs=in_specs,
        out_specs=pl.BlockSpec((tm, tn), lambda j, i: (i, j)),
        out_shape=jax.ShapeDtypeStruct((m, n_out), out_dtype),
        scratch_shapes=scratch,
        compiler_params=_params(("arbitrary", "arbitrary"), blocks,
                                _nbytes((k, tn), BF16) * cast_w + 2 * _nbytes((tm, tn), F32)),
        name=name,
    )(*args)


def _gateup_kernel(a_ref, wg_ref, wu_ref, o_ref, wgb_ref, wub_ref):
    @pl.when(pl.program_id(1) == 0)
    def _():
        wgb_ref[...] = wg_ref[...].astype(BF16)
        wub_ref[...] = wu_ref[...].astype(BF16)
    a = a_ref[...]
    g = _dot(a, wgb_ref[...])
    u = _dot(a, wub_ref[...])
    o_ref[...] = (g * _sigmoid(g) * u).astype(o_ref.dtype)


def gateup(a, wg, wu, tm_pref=1024, tn_pref=256):
    m, k = a.shape
    f = wg.shape[1]
    tm = _tile(m, tm_pref, 16)
    tn = _tile(f, tn_pref, LANE)
    blocks = [_nbytes((tm, k), BF16), 2 * _nbytes((k, tn), F32), _nbytes((tm, tn), BF16)]
    return pl.pallas_call(
        _gateup_kernel,
        grid=(f // tn, m // tm),
        in_specs=[pl.BlockSpec((tm, k), lambda j, i: (i, 0)),
                  pl.BlockSpec((k, tn), lambda j, i: (0, j)),
                  pl.BlockSpec((k, tn), lambda j, i: (0, j))],
        out_specs=pl.BlockSpec((tm, tn), lambda j, i: (i, j)),
        out_shape=jax.ShapeDtypeStruct((m, f), BF16),
        scratch_shapes=[pltpu.VMEM((k, tn), BF16), pltpu.VMEM((k, tn), BF16)],
        compiler_params=_params(("arbitrary", "arbitrary"), blocks,
                                2 * _nbytes((k, tn), BF16) + 3 * _nbytes((tm, tn), F32)),
        name="ffn_gateup",
    )(a, wg, wu)


def _block_diag_ones(n, block, dtype):
    r = lax.broadcasted_iota(jnp.int32, (n, n), 0) // block
    c = lax.broadcasted_iota(jnp.int32, (n, n), 1) // block
    return jnp.where(r == c, 1.0, 0.0).astype(dtype)


def _head_sums(xs):
    x = xs[0] if len(xs) == 1 else jnp.concatenate(xs, axis=0)
    w = x.shape[1]
    gl = UNIT_LANES if w % UNIT_LANES == 0 else LANE
    bd = _block_diag_ones(gl, RW_HEAD, BF16)
    hi = x.astype(BF16)
    lo = (x - hi.astype(F32)).astype(BF16)
    parts = [_dot(hi[:, s:s + gl], bd) + _dot(lo[:, s:s + gl], bd) for s in range(0, w, gl)]
    out = parts[0] if len(parts) == 1 else jnp.concatenate(parts, axis=1)
    n = xs[0].shape[0]
    return [out[i * n:(i + 1) * n] for i in range(len(xs))]


def _rw_prep_kernel(*refs, rw, seg, has_vfirst, n_prompt, t_prompt):
    if has_vfirst:
        (p_ref, prev_ref, vf_ref, mu_ref, w2_ref, w0_ref, a2_ref, a0_ref, g2_ref, kk_ref, ka_ref,
         v2_ref, v0_ref, r_out, w_out, k_out, v_out, a_out, b_out, g_out) = refs
    else:
        (p_ref, prev_ref, mu_ref, w2_ref, w0_ref, a2_ref, a0_ref, g2_ref, kk_ref, ka_ref,
         r_out, w_out, k_out, v_out, a_out, b_out, g_out) = refs
    tr = p_ref.shape[0]
    row0 = pl.program_id(0) * tr
    rows = lax.broadcasted_iota(jnp.int32, (tr, 1), 0)
    grow = rows + row0
    seq_start = jnp.logical_and(grow < n_prompt, grow % t_prompt == 0)

    def shifted_lerp(lo, width):
        p = p_ref[:, lo:lo + width]
        above = pltpu.roll(p, 1, axis=0)
        above = jnp.where(rows == 0, prev_ref[SUBLANE - 1:SUBLANE, lo:lo + width], above)
        above = jnp.where(seq_start, 0.0, above)
        return p + mu_ref[:, lo:lo + width] * (above - p)

    o_r, o_k, o_v, o_w, w_w, o_a, w_a, o_g, w_g, o_xv, w_xv = seg
    r = shifted_lerp(o_r, rw)
    kr = shifted_lerp(o_k, rw)
    vr = shifted_lerp(o_v, rw)
    xw = shifted_lerp(o_w, w_w)
    xa = shifted_lerp(o_a, w_a)
    xg = shifted_lerp(o_g, w_g)

    w_log = -_softplus(-(w0_ref[...] + _dot(jnp.tanh(xw).astype(BF16), w2_ref[...]))) - 0.5
    decay = jnp.exp(-jnp.exp(w_log))
    a = _sigmoid(a0_ref[...] + _dot(xa.astype(BF16), a2_ref[...]))
    g = _dot(_sigmoid(xg).astype(BF16), g2_ref[...])
    if has_vfirst:
        xv = shifted_lerp(o_xv, w_xv)
        vr = vr + (vf_ref[...] - vr) * _sigmoid(v0_ref[...] + _dot(xv.astype(BF16), v2_ref[...]))
    kk = kr * kk_ref[...]
    kk = kk * lax.rsqrt(jnp.maximum(_head_sums([kk * kk])[0], 1e-24))
    r_out[...] = r
    w_out[...] = decay
    k_out[...] = kr * (1.0 + (a - 1.0) * ka_ref[...])
    v_out[...] = vr
    a_out[...] = -kk
    b_out[...] = kk * a
    g_out[...] = g


def rw_prep(p_rw, seg, rw, mu, w2, w0, a2, a0, g2, kk, ka, v2, v0, v_first, n_prompt, t_prompt):
    m, n_rw = p_rw.shape
    tr = _tile(m, 128, SUBLANE)
    nsub = tr // SUBLANE
    has_vfirst = v_first is not None
    row = lambda w: pl.BlockSpec((1, w), lambda i: (0, 0))
    full = lambda x: pl.BlockSpec(x.shape, lambda i: (0, 0))
    tile = pl.BlockSpec((tr, rw), lambda i: (i, 0))
    in_specs = [pl.BlockSpec((tr, n_rw), lambda i: (i, 0)),
                pl.BlockSpec((SUBLANE, n_rw), lambda i: (jnp.maximum(i * nsub - 1, 0), 0))]
    args = [p_rw, p_rw]
    if has_vfirst:
        in_specs.append(tile)
        args.append(v_first)
    in_specs += [row(n_rw), full(w2), row(rw), full(a2), row(rw), full(g2), row(rw), row(rw)]
    args += [mu.reshape(1, n_rw), w2, w0.reshape(1, rw), a2, a0.reshape(1, rw), g2,
             kk.reshape(1, rw), ka.reshape(1, rw)]
    if has_vfirst:
        in_specs += [full(v2), row(rw)]
        args += [v2, v0.reshape(1, rw)]
    blocks = [_nbytes((tr + SUBLANE, n_rw), F32), (7 + has_vfirst) * _nbytes((tr, rw), F32),
              sum(_nbytes(x.shape, x.dtype) for x in (w2, a2, g2)) + (_nbytes(v2.shape, v2.dtype) if has_vfirst else 0)]
    outs = pl.pallas_call(
        functools.partial(_rw_prep_kernel, rw=rw, seg=seg, has_vfirst=has_vfirst,
                          n_prompt=n_prompt, t_prompt=t_prompt),
        grid=(m // tr,),
        in_specs=in_specs,
        out_specs=[tile] * 7,
        out_shape=[jax.ShapeDtypeStruct((m, rw), F32)] * 7,
        compiler_params=_params(("arbitrary",), blocks, 12 * _nbytes((tr, rw), F32)),
        name="rwkv_prep",
    )(*args)
    return outs


def _rw_scan_kernel(*refs, nb, n_groups, tc, t_start, split_in, layer, owns_stack):
    n_op = 7 * (nb if split_in else 1)
    op_refs = refs[:n_op]
    gnw_ref, gnb_ref, rk_ref, s0_ref = refs[n_op:n_op + 4]
    o_ref, sout_ref, s_scr, lhs_scr, y8_scr, y_scr = refs[-6:]
    c = pl.program_id(1)
    units = [(s, grp) for s in range(nb) for grp in range(n_groups)]

    def operand(kind, s):
        return (op_refs[kind * nb + s], 0) if split_in else (op_refs[kind], s * tc)

    @pl.when(c == 0)
    def _():
        for u, (s, grp) in enumerate(units):
            s_scr[u * RW_HEAD:(u + 1) * RW_HEAD, :] = jnp.concatenate(
                [s0_ref[s, UNIT_HEADS * grp + h] for h in range(UNIT_HEADS)], axis=1)

    bd = _block_diag_ones(UNIT_LANES, RW_HEAD, BF16)
    diag = (lax.broadcasted_iota(jnp.int32, (RW_HEAD, UNIT_LANES), 1) % RW_HEAD
            == lax.broadcasted_iota(jnp.int32, (RW_HEAD, UNIT_LANES), 0))

    def group_step(base, first):
        def row(kind, u, j):
            s, grp = units[u]
            ref, off = operand(kind, s)
            return ref[pl.ds(off + base, SUBLANE), grp * UNIT_LANES:(grp + 1) * UNIT_LANES][j:j + 1, :]

        n_u = len(units)
        q0 = 2 * n_u * RW_HEAD

        def put_sa_v(par, u, s, j):
            lhs_scr[par, 2 * u * RW_HEAD:(2 * u + 1) * RW_HEAD, :] = (s * row(4, u, j)).astype(BF16)
            lhs_scr[par, (2 * u + 1) * RW_HEAD:(2 * u + 2) * RW_HEAD, :] = (
                jnp.where(diag, row(3, u, j), 0.0).astype(BF16))

        if first:
            y8_scr[...] = jnp.zeros(y8_scr.shape, F32)
        for u in range(n_u):
            put_sa_v(0, u, s_scr[u * RW_HEAD:(u + 1) * RW_HEAD, :], first)
        bc = _dot(lhs_scr[0, :q0], bd)
        for j in range(first, SUBLANE):
            par = (j - first + 1) % 2
            last = j == SUBLANE - 1
            for u in range(n_u):
                rows = slice(u * RW_HEAD, (u + 1) * RW_HEAD)
                s = (s_scr[rows, :] * row(1, u, j)
                     + bc[2 * u * RW_HEAD:(2 * u + 1) * RW_HEAD] * row(5, u, j)
                     + bc[(2 * u + 1) * RW_HEAD:(2 * u + 2) * RW_HEAD] * row(2, u, j))
                s_scr[rows, :] = s
                lhs_scr[par, q0 + u * RW_HEAD:q0 + (u + 1) * RW_HEAD, :] = (s * row(0, u, j)).astype(BF16)
                if not last:
                    put_sa_v(par, u, s, j + 1)
            bc = _dot(lhs_scr[par, q0:] if last else lhs_scr[par], bd)
            y_bc = bc if last else bc[q0:]
            for u in range(n_u):
                y_row = jnp.sum(jnp.where(diag, y_bc[u * RW_HEAD:(u + 1) * RW_HEAD], 0.0), axis=0, keepdims=True)
                y8_scr[u * SUBLANE + j:u * SUBLANE + j + 1, :] = y_row
        for u, (s, grp) in enumerate(units):
            y_scr[pl.ds(s * tc + base, SUBLANE), grp * UNIT_LANES:(grp + 1) * UNIT_LANES] = (
                y8_scr[u * SUBLANE:(u + 1) * SUBLANE, :])

    if tc == SUBLANE:
        group_step(0, t_start)
    else:
        assert t_start == 0 and tc % SUBLANE == 0

        def body(i, carry):
            group_step(pl.multiple_of(i * SUBLANE, SUBLANE), 0)
            return carry

        lax.fori_loop(0, tc // SUBLANE, body, 0)

    for s in range(nb):
        def full(kind):
            ref, off = operand(kind, s)
            return ref[off:off + tc, :]
        y = y_scr[s * tc:(s + 1) * tc, :]
        mean, rk_sum = _head_sums([y, full(0) * full(2) * rk_ref[...]])
        yc = y - mean * (1.0 / RW_HEAD)
        var = _head_sums([yc * yc])[0] * (1.0 / RW_HEAD)
        yn = yc * lax.rsqrt(var + GN_EPS) * gnw_ref[...] + gnb_ref[...]
        o_ref[s] = ((yn + rk_sum * full(3)) * full(6)).astype(o_ref.dtype)

    @pl.when(c == pl.num_programs(1) - 1)
    def _():
        out = _state_out(sout_ref, layer, owns_stack)
        for u, (s, grp) in enumerate(units):
            st = s_scr[u * RW_HEAD:(u + 1) * RW_HEAD, :]
            for h in range(UNIT_HEADS):
                out[s, UNIT_HEADS * grp + h] = st[:, h * RW_HEAD:(h + 1) * RW_HEAD]


def _state_io(s0, s0_layer, layer, depth, s_prev, nb, n_inputs):
    blk = (None, nb) + s0.shape[2:]
    zeros = (0,) * (s0.ndim - 2)
    in_spec = pl.BlockSpec(blk, lambda i, c: (s0_layer, i) + zeros)
    out_shape = jax.ShapeDtypeStruct((depth,) + s0.shape[1:], s0.dtype)
    if s_prev is None:
        out_spec = pl.BlockSpec((depth, nb) + s0.shape[2:], lambda i, c: (0, i) + zeros)
        return in_spec, out_spec, out_shape, [], [], {}
    out_spec = pl.BlockSpec(blk, lambda i, c: (layer, i) + zeros)
    return in_spec, out_spec, out_shape, [pl.BlockSpec(memory_space=pl.ANY)], [s_prev], {n_inputs: 1}


def _state_out(sout_ref, layer, owns_stack):
    if not owns_stack:
        return sout_ref
    for other in range(sout_ref.shape[0]):
        if other != layer:
            sout_ref[other] = jnp.zeros(sout_ref.shape[1:], sout_ref.dtype)
    return sout_ref.at[layer]


def rw_scan(ops, g, gn_w, gn_b, rk, s0, s0_layer, layer, depth, s_prev, row0, n_seq, t_seq, tc, t_start, nb):
    rw = ops[0].shape[1]
    n_heads = s0.shape[2]
    assert n_heads % UNIT_HEADS == 0
    n_groups = n_heads // UNIT_HEADS
    n_chunks = t_seq // tc
    split_in = n_chunks > 1
    assert n_seq % nb == 0 and (split_in or (row0 % (nb * tc) == 0 and tc == t_seq))
    if split_in:
        base = row0 // tc
        in_specs = [pl.BlockSpec((tc, rw), lambda i, c, s=s: (base + (i * nb + s) * n_chunks + c, 0))
                    for _ in range(7) for s in range(nb)]
        args = [x for x in (*ops, g) for _ in range(nb)]
    else:
        base = row0 // (nb * tc)
        in_specs = [pl.BlockSpec((nb * tc, rw), lambda i, c: (base + i, 0))] * 7
        args = [*ops, g]
    row = pl.BlockSpec((1, rw), lambda i, c: (0, 0))
    st_in, st_out, st_shape, prev_spec, prev_arg, alias = _state_io(
        s0, s0_layer, layer, depth, s_prev, nb, len(args) + 4)
    n_units = nb * n_groups
    scratch = [pltpu.VMEM((n_units * RW_HEAD, UNIT_LANES), F32),
               pltpu.VMEM((2, 3 * n_units * RW_HEAD, UNIT_LANES), BF16),
               pltpu.VMEM((n_units * SUBLANE, UNIT_LANES), F32),
               pltpu.VMEM((nb * tc, rw), F32)]
    blocks = [7 * _nbytes((nb * tc, rw), F32), _nbytes((nb * tc, rw), BF16),
              (1 + depth) * _nbytes((nb * n_heads, RW_HEAD, LANE), F32)]
    o, s_out = pl.pallas_call(
        functools.partial(_rw_scan_kernel, nb=nb, n_groups=n_groups, tc=tc, t_start=t_start, split_in=split_in,
                          layer=layer, owns_stack=s_prev is None),
        grid=(n_seq // nb, n_chunks),
        in_specs=in_specs + [row] * 3 + [st_in] + prev_spec,
        out_specs=[pl.BlockSpec((nb, tc, rw), lambda i, c: (i, c, 0)), st_out],
        out_shape=[jax.ShapeDtypeStruct((n_seq, t_seq, rw), BF16), st_shape],
        input_output_aliases=alias,
        scratch_shapes=scratch,
        compiler_params=_params(("arbitrary", "arbitrary"), blocks,
                                16 * _nbytes((n_units * RW_HEAD, UNIT_LANES), F32) + 8 * _nbytes((tc, rw), F32)),
        name="rwkv_scan",
    )(*args, gn_w.reshape(1, rw), gn_b.reshape(1, rw), rk.reshape(1, rw), s0, *prev_arg)
    return o.reshape(n_seq * t_seq, rw), s_out


def _gla_kernel(*refs, n_heads, dk, dv, t_start, layer, owns_stack):
    q_ref, k_ref, v_ref, go_ref, lr_ref, w2_ref, gb_ref, gn_ref, s0_ref = refs[:9]
    o_ref, sout_ref, s_scr = refs[-3:]
    c_rows = q_ref.shape[0]
    n = pl.program_id(1)

    @pl.when(n == 0)
    def _():
        s_scr[...] = s0_ref[0]

    rows = lax.broadcasted_iota(jnp.int32, (c_rows, 1), 0)
    valid = rows >= t_start
    tri = (lax.broadcasted_iota(jnp.int32, (c_rows, c_rows), 0)
           >= lax.broadcasted_iota(jnp.int32, (c_rows, c_rows), 1))
    tri_f = jnp.where(tri, 1.0, 0.0).astype(F32)
    lr = lr_ref[...]
    mid = max(c_rows // 2 - 1, 0)
    scale = dk ** -0.5
    for h in range(n_heads):
        ks = slice(h * dk, (h + 1) * dk)
        vs = slice(h * dv, (h + 1) * dv)
        x = _dot(lr, w2_ref[:, ks], precision=HIGHEST) + gb_ref[:, ks]
        log_a = (jnp.minimum(x, 0.0) - jnp.log(1.0 + jnp.exp(-jnp.abs(x)))) * (1.0 / GLA_GATE_TAU)
        log_a = jnp.where(valid, log_a, 0.0)
        b = _dot(tri_f, log_a, precision=HIGHEST)
        b_last = b[c_rows - 1:c_rows, :]
        b_mid = b[mid:mid + 1, :]
        q = q_ref[:, ks] * scale
        k = jnp.where(valid, k_ref[:, ks], 0.0)
        v = v_ref[:, vs].astype(BF16)
        qi = (q * jnp.exp(b - b_mid)).astype(BF16)
        ki = (k * jnp.exp(b_mid - b)).astype(BF16)
        att = lax.dot_general(qi, ki, (((1,), (1,)), ((), ())), preferred_element_type=F32)
        att = jnp.where(tri, att, 0.0).astype(BF16)
        s = s_scr[h]
        o = _dot(att, v) + _dot((q * jnp.exp(b)).astype(BF16), s.astype(BF16))
        kd = (k * jnp.exp(b_last - b)).astype(BF16)
        upd = lax.dot_general(kd, v, (((0,), (0,)), ((), ())), preferred_element_type=F32)
        dcol = jnp.transpose(jnp.broadcast_to(jnp.exp(b_last), (LANE, dk)))
        s_scr[h] = s * jnp.concatenate([dcol] * (dv // LANE), axis=1) + upd
        on = o * lax.rsqrt(jnp.mean(o * o, axis=-1, keepdims=True) + HEAD_NORM_EPS)
        go = go_ref[:, vs]
        o_ref[:, vs] = (on * gn_ref[:, vs] * (go * _sigmoid(go))).astype(o_ref.dtype)

    @pl.when(n == pl.num_programs(1) - 1)
    def _():
        _state_out(sout_ref, layer, owns_stack)[0] = s_scr[...]


def gla(p_gla, p_rw, lr_block, w2p, gb, gn, s0, s0_layer, layer, depth, s_prev, row0, n_seq, t_seq, c_rows, t_start):
    _, _, n_heads, dk, dv = s0.shape
    qk, gw = n_heads * dk, n_heads * dv
    n_chunks = t_seq // c_rows
    base = row0 // c_rows
    rmap = lambda col: (lambda i, n: (base + i * n_chunks + n, col))
    st_in, st_out, st_shape, prev_spec, prev_arg, alias = _state_io(s0, s0_layer, layer, depth, s_prev, 1, 9)
    const = lambda shape: pl.BlockSpec(shape, lambda i, n: (0, 0))
    blocks = [_nbytes((c_rows, 2 * qk + 2 * gw + LANE), F32), _nbytes((LANE + 2, qk), F32),
              _nbytes((c_rows, gw), BF16), (1 + depth) * _nbytes((n_heads, dk, dv), F32)]
    o, s_out = pl.pallas_call(
        functools.partial(_gla_kernel, n_heads=n_heads, dk=dk, dv=dv, t_start=t_start,
                          layer=layer, owns_stack=s_prev is None),
        grid=(n_seq, n_chunks),
        in_specs=[pl.BlockSpec((c_rows, qk), rmap(0)),
                  pl.BlockSpec((c_rows, qk), rmap(1)),
                  pl.BlockSpec((c_rows, gw), rmap(2 * qk // gw)),
                  pl.BlockSpec((c_rows, gw), rmap(2 * qk // gw + 1)),
                  pl.BlockSpec((c_rows, LANE), rmap(lr_block)),
                  const((LANE, qk)), const((1, qk)), const((1, gw)), st_in] + prev_spec,
        out_specs=[pl.BlockSpec((c_rows, gw), lambda i, n: (i * n_chunks + n, 0)), st_out],
        out_shape=[jax.ShapeDtypeStruct((n_seq * t_seq, gw), BF16), st_shape],
        input_output_aliases=alias,
        scratch_shapes=[pltpu.VMEM((n_heads, dk, dv), F32)],
        compiler_params=_params(("arbitrary", "arbitrary"), blocks,
                                _nbytes((n_heads, dk, dv), F32) + 6 * _nbytes((dk, dv), F32)),
        name="gla",
    )(p_gla, p_gla, p_gla, p_gla, p_rw, w2p, gb.reshape(1, qk), gn.reshape(1, gw), s0, *prev_arg)
    return o, s_out


def _router_kernel(x_ref, g_ref, wr_ref, h_ref, info_ref, *, n_experts):
    x = x_ref[...]
    h = x * lax.rsqrt(jnp.mean(x * x, axis=-1, keepdims=True) + NORM_EPS) * g_ref[...]
    half = h.shape[1] // 2
    lo = lax.bitcast_convert_type(h[:, :half].astype(BF16).astype(F32), jnp.uint32)
    hi = lax.bitcast_convert_type(h[:, half:].astype(BF16).astype(F32), jnp.uint32)
    h_ref[...] = (lo >> 16) | (hi & jnp.uint32(0xFFFF0000))
    logits = _dot(h, wr_ref[...], precision=HIGHEST)
    lane = lax.broadcasted_iota(jnp.int32, logits.shape, 1)
    neg = jnp.float32(-jnp.inf)
    logits = jnp.where(lane < n_experts, logits, neg)
    m1 = jnp.max(logits, axis=-1, keepdims=True)
    i1 = jnp.min(jnp.where(logits == m1, lane, LANE), axis=-1, keepdims=True)
    rest = jnp.where(lane == i1, neg, logits)
    m2 = jnp.max(rest, axis=-1, keepdims=True)
    i2 = jnp.min(jnp.where(rest == m2, lane, LANE), axis=-1, keepdims=True)
    e2 = jnp.exp(m2 - m1)
    g1 = 1.0 / (1.0 + e2)
    g2 = e2 / (1.0 + e2)
    info = jnp.where(lane == 0, i1.astype(F32),
                     jnp.where(lane == 1, i2.astype(F32),
                               jnp.where(lane == 2, g1, jnp.where(lane == 3, g2, 0.0))))
    info_ref[...] = info


def moe_router(x, g, router):
    m, d = x.shape
    n_experts = router.shape[1]
    tr = _tile(m, 256, SUBLANE)
    wr = jnp.zeros((d, LANE), F32).at[:, :n_experts].set(router)
    blocks = [_nbytes((tr, d), F32), _nbytes((d, LANE), F32), _nbytes((tr, d // 2), jnp.uint32)]
    return pl.pallas_call(
        functools.partial(_router_kernel, n_experts=n_experts),
        grid=(m // tr,),
        in_specs=[pl.BlockSpec((tr, d), lambda i: (i, 0)), pl.BlockSpec((1, d), lambda i: (0, 0)),
                  pl.BlockSpec((d, LANE), lambda i: (0, 0))],
        out_specs=[pl.BlockSpec((tr, d // 2), lambda i: (i, 0)), pl.BlockSpec((tr, LANE), lambda i: (i, 0))],
        out_shape=[jax.ShapeDtypeStruct((m, d // 2), jnp.uint32), jax.ShapeDtypeStruct((m, LANE), F32)],
        compiler_params=_params(("arbitrary",), blocks, 4 * _nbytes((tr, d), F32)),
        name="moe_router",
    )(x, g.reshape(1, d), wr)


def _gather_kernel(idx_ref, src_ref, out_ref, buf, sem):
    tg, w = buf.shape

    def copy(r):
        return pltpu.make_async_copy(src_ref.at[pl.ds(idx_ref[0, 0, r], 1)], buf.at[pl.ds(r, 1)], sem)

    def wait(r, carry):
        copy(r).wait()
        return carry

    for r in range(tg):
        copy(r).start(priority=r % 2)
    lax.fori_loop(0, tg, wait, 0)
    words = buf[...]
    out_ref[:, :w] = lax.bitcast_convert_type(words << 16, F32).astype(BF16)
    out_ref[:, w:] = lax.bitcast_convert_type(words & jnp.uint32(0xFFFF0000), F32).astype(BF16)


def gather_rows(src, idx, tg=256):
    n = idx.shape[0]
    tg = _tile(n, tg, 16)
    w = src.shape[1]
    return pl.pallas_call(
        _gather_kernel,
        grid=(n // tg,),
        in_specs=[pl.BlockSpec((1, 1, tg), lambda i: (i, 0, 0), memory_space=pltpu.SMEM),
                  pl.BlockSpec(memory_space=pl.ANY)],
        out_specs=pl.BlockSpec((tg, 2 * w), lambda i: (i, 0)),
        out_shape=jax.ShapeDtypeStruct((n, 2 * w), BF16),
        scratch_shapes=[pltpu.VMEM((tg, w), src.dtype), pltpu.SemaphoreType.DMA(())],
        compiler_params=_params(("arbitrary",), [_nbytes((tg, 2 * w), BF16)], 4 * _nbytes((tg, w), src.dtype)),
        name="moe_gather",
    )(idx.reshape(n // tg, 1, tg), src)


def _moe_gateup_kernel(te_ref, tf_ref, tv_ref, ts_ref, a_ref, wg_ref, wu_ref, o_ref, wgb_ref, wub_ref):
    t = pl.program_id(1)

    @pl.when(tf_ref[t] == 1)
    def _():
        wgb_ref[...] = wg_ref[...].astype(BF16)
        wub_ref[...] = wu_ref[...].astype(BF16)

    @pl.when(tv_ref[t] == 1)
    def _():
        a = a_ref[...]
        g = _dot(a, wgb_ref[...])
        u = _dot(a, wub_ref[...])
        o_ref[...] = (g * _sigmoid(g) * u).astype(o_ref.dtype)

    @pl.when(tv_ref[t] == 0)
    def _():
        o_ref[...] = jnp.zeros(o_ref.shape, o_ref.dtype)


def moe_gateup(xs, wg, wu, tile_expert, tile_first, tile_valid, tile_src, tm, tn_pref=512):
    r, d = xs.shape
    f = wg.shape[2]
    tn = _tile(f, tn_pref, LANE)
    wspec = pl.BlockSpec((None, d, tn), lambda j, t, te, tf, tv, ts: (te[t], 0, j))
    blocks = [_nbytes((tm, d), BF16), 2 * _nbytes((d, tn), F32), _nbytes((tm, tn), BF16)]
    return pl.pallas_call(
        _moe_gateup_kernel,
        grid_spec=pltpu.PrefetchScalarGridSpec(
            num_scalar_prefetch=4,
            grid=(f // tn, r // tm),
            in_specs=[pl.BlockSpec((tm, d), lambda j, t, te, tf, tv, ts: (ts[t], 0)), wspec, wspec],
            out_specs=pl.BlockSpec((tm, tn), lambda j, t, te, tf, tv, ts: (t, j)),
            scratch_shapes=[pltpu.VMEM((d, tn), BF16), pltpu.VMEM((d, tn), BF16)]),
        out_shape=jax.ShapeDtypeStruct((r, f), BF16),
        compiler_params=_params(("arbitrary", "arbitrary"), blocks,
                                2 * _nbytes((d, tn), BF16) + 3 * _nbytes((tm, tn), F32)),
        name="moe_gateup",
    )(tile_expert, tile_first, tile_valid, tile_src, xs, wg, wu)


def _moe_down_kernel(te_ref, tf_ref, tv_ref, ts_ref, a_ref, w_hbm, o_ref, wb_ref, stage, sem, *, nk, tk, tn):
    j = pl.program_id(0)
    t = pl.program_id(1)

    def chunk_copy(c):
        src = w_hbm.at[te_ref[t], pl.ds(c * tk, tk), pl.ds(pl.multiple_of(j * tn, LANE), tn)]
        return pltpu.make_async_copy(src, stage.at[c % 2], sem.at[c % 2])

    def chunk_dot(c):
        return _dot(a_ref[:, c * tk:(c + 1) * tk], wb_ref[c])

    @pl.when(tf_ref[t] == 1)
    def _():
        chunk_copy(0).start()
        acc = None
        for c in range(nk):
            if c + 1 < nk:
                chunk_copy(c + 1).start()
            chunk_copy(c).wait()
            wb_ref[c] = stage[c % 2].astype(BF16)
            acc = chunk_dot(c) if acc is None else acc + chunk_dot(c)
        o_ref[...] = acc

    @pl.when(jnp.logical_and(tf_ref[t] == 0, tv_ref[t] == 1))
    def _():
        acc = chunk_dot(0)
        for c in range(1, nk):
            acc = acc + chunk_dot(c)
        o_ref[...] = acc

    @pl.when(tv_ref[t] == 0)
    def _():
        o_ref[...] = jnp.zeros(o_ref.shape, o_ref.dtype)


def moe_down(act, wd, tile_expert, tile_first, tile_valid, tile_src, tm, tn_pref=512, tk_pref=1792):
    r, f = act.shape
    d = wd.shape[2]
    tn = _tile(d, tn_pref, LANE)
    tk = _tile(f, tk_pref, LANE)
    nk = f // tk
    blocks = [_nbytes((tm, f), BF16), _nbytes((tm, tn), F32)]
    return pl.pallas_call(
        functools.partial(_moe_down_kernel, nk=nk, tk=tk, tn=tn),
        grid_spec=pltpu.PrefetchScalarGridSpec(
            num_scalar_prefetch=4,
            grid=(d // tn, r // tm),
            in_specs=[pl.BlockSpec((tm, f), lambda j, t, te, tf, tv, ts: (ts[t], 0)),
                      pl.BlockSpec(memory_space=pl.ANY)],
            out_specs=pl.BlockSpec((tm, tn), lambda j, t, te, tf, tv, ts: (t, j)),
            scratch_shapes=[pltpu.VMEM((nk, tk, tn), BF16), pltpu.VMEM((2, tk, tn), F32),
                            pltpu.SemaphoreType.DMA((2,))]),
        out_shape=jax.ShapeDtypeStruct((r, d), F32),
        compiler_params=_params(("arbitrary", "arbitrary"), blocks,
                                _nbytes((f, tn), BF16) + 2 * _nbytes((tk, tn), F32) + 2 * _nbytes((tm, tn), F32)),
        name="moe_down",
    )(tile_expert, tile_first, tile_valid, tile_src, act, wd)


def _combine_kernel(p1_ref, p2_ref, x_ref, info_ref, ys_ref, o_ref, buf1, buf2, sem):
    tc = x_ref.shape[0]

    def copies(r):
        return (pltpu.make_async_copy(ys_ref.at[pl.ds(p1_ref[0, 0, r], 1)], buf1.at[pl.ds(r, 1)], sem.at[0]),
                pltpu.make_async_copy(ys_ref.at[pl.ds(p2_ref[0, 0, r], 1)], buf2.at[pl.ds(r, 1)], sem.at[1]))

    def wait(r, carry):
        c1, c2 = copies(r)
        c1.wait()
        c2.wait()
        return carry

    for r in range(tc):
        c1, c2 = copies(r)
        c1.start(priority=0)
        c2.start(priority=1)
    lax.fori_loop(0, tc, wait, 0)
    info = info_ref[...]
    o_ref[...] = x_ref[...] + info[:, 2:3] * buf1[...] + info[:, 3:4] * buf2[...]


def moe_combine(x, info, ys, pos1, pos2, tc=128):
    m, d = x.shape
    tc = _tile(m, tc, SUBLANE)
    idx = pl.BlockSpec((1, 1, tc), lambda i: (i, 0, 0), memory_space=pltpu.SMEM)
    blocks = [2 * _nbytes((tc, d), F32), _nbytes((tc, LANE), F32)]
    return pl.pallas_call(
        _combine_kernel,
        grid=(m // tc,),
        in_specs=[idx, idx, pl.BlockSpec((tc, d), lambda i: (i, 0)), pl.BlockSpec((tc, LANE), lambda i: (i, 0)),
                  pl.BlockSpec(memory_space=pl.ANY)],
        out_specs=pl.BlockSpec((tc, d), lambda i: (i, 0)),
        out_shape=jax.ShapeDtypeStruct((m, d), F32),
        scratch_shapes=[pltpu.VMEM((tc, d), F32), pltpu.VMEM((tc, d), F32), pltpu.SemaphoreType.DMA((2,))],
        compiler_params=_params(("arbitrary",), blocks, 4 * _nbytes((tc, d), F32)),
        name="moe_combine",
    )(pos1.reshape(m // tc, 1, tc), pos2.reshape(m // tc, 1, tc), x, info, ys)


def moe_ffn(x, norm_g, router, wg, wu, wd, row_valid, n_valid, tm=384):
    m, d = x.shape
    n_experts = router.shape[1]
    h_packed, info = moe_router(x, norm_g, router)

    none = n_experts
    e1 = jnp.where(row_valid, info[:, 0].astype(jnp.int32), none)
    e2 = jnp.where(row_valid, info[:, 1].astype(jnp.int32), none)
    e_all = jnp.concatenate([e1, e2])
    onehot = (e_all[:, None] == jnp.arange(n_experts, dtype=jnp.int32)[None, :]).astype(jnp.int32)
    counts = jnp.sum(onehot, axis=0)
    rank = jnp.sum((jnp.cumsum(onehot, axis=0) - onehot) * onehot, axis=1)
    padded = (counts + tm - 1) // tm * tm
    ends = jnp.cumsum(padded)
    starts = ends - padded
    n_rows = _round_up(2 * n_valid + n_experts * (tm - 1), tm)
    n_tiles = n_rows // tm
    pos = jnp.where(e_all < none, jnp.take(starts, jnp.minimum(e_all, none - 1)) + rank, n_rows)
    tok = jnp.tile(jnp.arange(m, dtype=jnp.int32), 2)
    row_src = jnp.zeros((n_rows,), jnp.int32).at[pos].set(tok, mode="drop")
    tile_start = jnp.arange(n_tiles, dtype=jnp.int32) * tm
    tile_valid = (tile_start < ends[-1]).astype(jnp.int32)
    last_tile = jnp.maximum(ends[-1] // tm - 1, 0)
    tile_expert = jnp.searchsorted(ends, jnp.minimum(tile_start, last_tile * tm), side="right").astype(jnp.int32)
    tile_expert = jnp.minimum(tile_expert, n_experts - 1)
    prev_expert = jnp.concatenate([jnp.full((1,), -1, jnp.int32), tile_expert[:-1]])
    tile_first = jnp.logical_and(tile_expert != prev_expert, tile_valid == 1).astype(jnp.int32)
    tile_first = tile_first.at[0].set(1)
    tile_src = jnp.minimum(jnp.arange(n_tiles, dtype=jnp.int32), last_tile).astype(jnp.int32)
    pos_safe = jnp.where(pos < n_rows, pos, 0)
    gates = jnp.where(row_valid[:, None], info, 0.0)

    xs = gather_rows(h_packed, row_src)
    act = moe_gateup(xs, wg, wu, tile_expert, tile_first, tile_valid, tile_src, tm)
    ys = moe_down(act, wd, tile_expert, tile_first, tile_valid, tile_src, tm)
    return moe_combine(x, gates, ys, pos_safe[:m], pos_safe[m:])


def _pad_cols(w, width):
    return w if w.shape[-1] == width else jnp.pad(w, [(0, 0)] * (w.ndim - 1) + [(0, width - w.shape[-1])])


def _pad_rows(w, rows):
    return w if w.shape[0] == rows else jnp.pad(w, [(0, rows - w.shape[0])] + [(0, 0)] * (w.ndim - 1))


def kernel(x_prompt, x_sample, state_gla, state_rwkv, state_shift, norm_mix, norm_ffn, norm_final,
           w_in_first, w_in_rest, mu_first, mu_rest, gla_w2, gla_b, gla_norm,
           rw_w2, rw_w0, rw_a2, rw_a0, rw_g2, rw_kk, rw_ka, rw_rk, rw_gn_w, rw_gn_b, rw_v2, rw_v0,
           w_out, ffn_w_gate, ffn_w_up, ffn_w_down, moe_router_w, moe_w_gate, moe_w_up, moe_w_down):
    batch, seq, d = x_prompt.shape
    dec_batch, dec_seq, _ = x_sample.shape
    depth = norm_mix.shape[0]
    _, _, gla_heads, dk, dv = state_gla.shape
    rw_heads = state_rwkv.shape[2]
    assert state_rwkv.shape[3] == RW_HEAD and state_rwkv.shape[4] == RW_HEAD and rw_heads % 2 == 0
    qk, gw, rw = gla_heads * dk, gla_heads * dv, rw_heads * RW_HEAD
    rank = gla_w2.shape[1]
    p_gla = 2 * qk + 2 * gw + rank
    assert dk % LANE == 0 and dv % LANE == 0 and (2 * qk) % gw == 0 and rank <= LANE

    group = _round_up(dec_seq + 1, SUBLANE)
    t_start = group - dec_seq
    n_prompt = batch * seq
    m = n_prompt + dec_batch * group
    prompt_chunk = _tile(seq, 64, SUBLANE)
    scan_chunk = _tile(seq, 64, SUBLANE)
    sample_rows = jnp.arange(dec_batch * group, dtype=jnp.int32) % group >= t_start
    row_valid = jnp.concatenate([jnp.ones((n_prompt,), bool), sample_rows])

    x = jnp.concatenate([
        x_prompt.reshape(n_prompt, d),
        jnp.concatenate([jnp.zeros((dec_batch, t_start, d), F32), x_sample], axis=1).reshape(dec_batch * group, d)])

    lw, la, lg = rw_w2.shape[1], rw_a2.shape[1], rw_g2.shape[1]
    lv = rw_v2.shape[1]
    pw, pa, pg, pv = (_round_up(n, LANE) for n in (lw, la, lg, lv))
    o_w = 3 * rw
    o_a = o_w + pw
    o_g = o_a + pa
    o_xv = o_g + pg
    o_lr = o_xv + pv
    n_rw = _round_up(o_lr + LANE, 512)
    seg = (0, rw, 2 * rw, o_w, pw, o_a, pa, o_g, pg, o_xv, pv)

    def repack(wt, mu_l, has_v):
        wr = wt[p_gla:]
        parts_w, parts_mu = [], []
        src = 0
        for width, padded in ((3 * rw, 3 * rw), (lw, pw), (la, pa), (lg, pg)) + (((lv, pv),) if has_v else ()):
            parts_w.append(_pad_rows(wr[src:src + width], padded))
            parts_mu.append(_pad_cols(mu_l[src:src + width], padded))
            src += width
        if not has_v:
            parts_w.append(jnp.zeros((pv, d), F32))
            parts_mu.append(jnp.zeros((pv,), F32))
        parts_w.append(_pad_rows(wt[p_gla - rank:p_gla], n_rw - o_lr))
        parts_mu.append(jnp.zeros((n_rw - o_lr,), F32))
        return jnp.concatenate(parts_w, axis=0).astype(BF16), jnp.concatenate(parts_mu)

    zeros_gla = jnp.zeros((1, batch, gla_heads, dk, dv), F32)
    zeros_rw = jnp.zeros((1, batch, rw_heads, RW_HEAD, RW_HEAD), F32)
    last_rows = jnp.concatenate([jnp.arange(batch, dtype=jnp.int32) * seq + seq - 1,
                                 n_prompt + jnp.arange(dec_batch, dtype=jnp.int32) * group + group - 1])
    n_last = _round_up(batch + dec_batch, SUBLANE)
    prev_rows = n_prompt + jnp.arange(dec_batch, dtype=jnp.int32) * group + t_start - 1

    new_gla_p = new_gla_s = new_rw_p = new_rw_s = None
    new_shift = []
    v_first = None
    for l in range(depth):
        h = rmsnorm(x, norm_mix[l], BF16).at[prev_rows].set(state_shift[l].astype(BF16))
        x_last = _pad_rows(jnp.take(x, last_rows, axis=0), n_last)
        new_shift.append(rmsnorm(x_last, norm_mix[l], F32)[:batch + dec_batch])

        if l == 0:
            wt, mu_l = w_in_first.T, mu_first
        else:
            wt, mu_l = w_in_rest[l - 1].T, mu_rest[l - 1]
        w_rw, mu_p = repack(wt, mu_l, l > 0)
        pg_all = matmul([h], wt, 2 * qk + 2 * gw, w_t=True, name="w_in_gla")
        pr_all = matmul([h], w_rw, n_rw, w_t=True, name="w_in_rwkv")

        ops = rw_prep(pr_all, seg, rw, mu_p,
                      _pad_rows(rw_w2[l], pw).astype(BF16), rw_w0[l], _pad_rows(rw_a2[l], pa).astype(BF16), rw_a0[l],
                      _pad_rows(rw_g2[l], pg).astype(BF16), rw_kk[l], rw_ka[l],
                      _pad_rows(rw_v2[l - 1], pv).astype(BF16) if l > 0 else None,
                      rw_v0[l - 1] if l > 0 else None, v_first, n_prompt, seq)
        r_, w_, k_, v_, a_, b_, g_ = ops
        if l == 0:
            v_first = v_
        scan_args = ((r_, w_, k_, v_, a_, b_), g_, rw_gn_w[l], rw_gn_b[l], rw_rk[l].reshape(rw))
        o_rw_p, new_rw_p = rw_scan(*scan_args, zeros_rw, 0, l, depth, new_rw_p,
                                   0, batch, seq, scan_chunk, 0, _tile(batch, 4, 1))
        o_rw_s, new_rw_s = rw_scan(*scan_args, state_rwkv, l, l, depth, new_rw_s,
                                   n_prompt, dec_batch, group, group, t_start, _tile(dec_batch, 4, 1))

        w2p = _pad_rows(gla_w2[l], LANE)
        gla_args = (pg_all, pr_all, o_lr // LANE, w2p, gla_b[l], gla_norm[l])
        o_gla_p, new_gla_p = gla(*gla_args, zeros_gla, 0, l, depth, new_gla_p, 0, batch, seq, prompt_chunk, 0)
        o_gla_s, new_gla_s = gla(*gla_args, state_gla, l, l, depth, new_gla_s,
                                 n_prompt, dec_batch, group, group, t_start)

        o_gla = jnp.concatenate([o_gla_p, o_gla_s])
        o_rw = jnp.concatenate([o_rw_p, o_rw_s])
        x = matmul([o_gla, o_rw], w_out, d, res=x, layer=l, name="w_out")

        i = l // 2
        if l % 2 == 0:
            h2 = rmsnorm(x, norm_ffn[l], BF16)
            act = gateup(h2, ffn_w_gate[i], ffn_w_up[i])
            x = matmul([act], ffn_w_down[i].astype(BF16), d, res=x, tm_pref=512, name="ffn_down")
        else:
            x = moe_ffn(x, norm_ffn[l], moe_router_w[i], moe_w_gate[i], moe_w_up[i], moe_w_down[i],
                        row_valid, n_prompt + dec_batch * dec_seq)

    y_prompt = rmsnorm(x, norm_final, F32, 0, n_prompt).reshape(batch, seq, d)
    y_sample = rmsnorm(x, norm_final, F32, n_prompt, m - n_prompt).reshape(dec_batch, group, d)[:, t_start:]
    shift = jnp.stack(new_shift)
    return (y_prompt, y_sample, new_gla_p, new_gla_s, new_rw_p, new_rw_s, shift[:, :batch], shift[:, batch:])
```
